```python
import math
import jax, jax.numpy as jnp
from jax import lax
import numpy as np

D_MODEL = 1024
BATCH = 8
SEQ = 8192
DEPTH = 2

CTX_LEN = 256
GRID_W = 64
QBLOCK = 128
WINDOW = 128
ROPE_THETA = 10000.0
EPS = 1e-6
NEG_INF = -1e30
D_FF = 2816
N_SUBLAYERS = 3
N_BRANCH = 4

HEAD_DIM = 64
GQA_HEADS = 4
GQA_KV_HEADS = 2
MLA_HEADS = 4
MLA_Q_RANK = 256
MLA_KV_RANK = 128
MLA_NOPE = 64
MLA_ROPE = 32
MLA_V = 64
DIFF_HEADS = 4
DIFF_QK = 32
DIFF_V = 2 * DIFF_QK
SWA_HEADS = 4
SWA_KV_HEADS = 2
BRANCH_W = 256

GQA_COLS = (GQA_HEADS + 2 * GQA_KV_HEADS) * HEAD_DIM
MLA_COLS = MLA_Q_RANK + MLA_KV_RANK + MLA_ROPE
DIFF_COLS = DIFF_HEADS * (4 * DIFF_QK + DIFF_V)
SWA_COLS = (SWA_HEADS + 2 * SWA_KV_HEADS) * HEAD_DIM
GATE_COLS = N_BRANCH * D_MODEL
IN_SPLITS = [GQA_COLS, GQA_COLS + MLA_COLS, GQA_COLS + MLA_COLS + DIFF_COLS,
             GQA_COLS + MLA_COLS + DIFF_COLS + SWA_COLS]
IN_COLS = IN_SPLITS[-1] + GATE_COLS

GQA_SCALE = HEAD_DIM ** -0.5
MLA_SCALE = (MLA_NOPE + MLA_ROPE) ** -0.5
DIFF_SCALE = DIFF_QK ** -0.5

kernel_name = 'hybrid_parallel_mixer_dit_block'


def rms_norm(x, g):
    xf = x.astype(jnp.float32)
    y = xf * lax.rsqrt(jnp.mean(xf * xf, axis=-1, keepdims=True) + EPS)
    return (y * g.astype(jnp.float32)).astype(x.dtype)


def adaln(x, g, shift, scale):
    return rms_norm(x, g) * (1 + scale[:, None, :]) + shift[:, None, :]


def swiglu(x, w_gate, w_up, w_down):
    return (jax.nn.silu(x @ w_gate) * (x @ w_up)) @ w_down


def ffn_half(x, mod_s, g_pre, g_post, w_gate, w_up, w_down):
    u = adaln(x, g_pre, mod_s[:, 0], mod_s[:, 1])
    return x + 0.5 * mod_s[:, 2][:, None, :] * rms_norm(swiglu(u, w_gate, w_up, w_down), g_post)


def axial_rope_tables(n_tokens, rot_dim):
    rows = n_tokens // GRID_W
    row = jnp.repeat(jnp.arange(rows, dtype=jnp.float32), GRID_W)
    col = jnp.tile(jnp.arange(GRID_W, dtype=jnp.float32), rows)
    half = rot_dim // 2
    inv_freq = ROPE_THETA ** (-jnp.arange(0, half, 2, dtype=jnp.float32) / half)
    ang_r = row[:, None] * inv_freq[None, :]
    ang_c = col[:, None] * inv_freq[None, :]
    return (jnp.cos(ang_r), jnp.sin(ang_r), jnp.cos(ang_c), jnp.sin(ang_c))


def _rotate(x, cos, sin):
    n = x.shape[-1] // 2
    x1, x2 = x[..., :n], x[..., n:]
    cos, sin = cos[:, None, :], sin[:, None, :]
    return jnp.concatenate([x1 * cos - x2 * sin, x2 * cos + x1 * sin], axis=-1)


def apply_axial_rope(x, rope):
    if rope is None:
        return x
    cos_r, sin_r, cos_c, sin_c = rope
    h = x.shape[-1] // 2
    out = jnp.concatenate([_rotate(x[..., :h], cos_r, sin_r), _rotate(x[..., h:], cos_c, sin_c)], axis=-1)
    return out.astype(x.dtype)


def q_heads(q, n_kv):
    B, T, H, d = q.shape
    return q.reshape(B, T, n_kv, H // n_kv, d).transpose(0, 2, 3, 1, 4)


def kv_heads(k):
    return k.transpose(0, 2, 1, 3)


def merge_heads(o):
    B, K, G, T, d = o.shape
    return o.transpose(0, 3, 1, 2, 4).reshape(B, T, K * G * d)


def gqa_proj(h, n_heads, n_kv, rope, q_gain=None, k_gain=None):
    B, T, _ = h.shape
    q, k, v = jnp.split(h, [n_heads * HEAD_DIM, (n_heads + n_kv) * HEAD_DIM], axis=-1)
    q = q.reshape(B, T, n_heads, HEAD_DIM)
    k = k.reshape(B, T, n_kv, HEAD_DIM)
    v = v.reshape(B, T, n_kv, HEAD_DIM)
    if q_gain is not None:
        q = rms_norm(q, q_gain)
        k = rms_norm(k, k_gain)
    q = apply_axial_rope(q, rope)
    k = apply_axial_rope(k, rope)
    return q_heads(q, n_kv), kv_heads(k), kv_heads(v)


def mla_proj(h, q_gain, kv_gain, w_uq, w_ukv, rope):
    B, T, _ = h.shape
    c_q, c_kv, k_pe = jnp.split(h, [MLA_Q_RANK, MLA_Q_RANK + MLA_KV_RANK], axis=-1)
    q = (rms_norm(c_q, q_gain) @ w_uq).reshape(B, T, MLA_HEADS, MLA_NOPE + MLA_ROPE)
    kv = (rms_norm(c_kv, kv_gain) @ w_ukv).reshape(B, T, MLA_HEADS, MLA_NOPE + MLA_V)
    q_nope, q_pe = q[..., :MLA_NOPE], q[..., MLA_NOPE:]
    k_nope, v = kv[..., :MLA_NOPE], kv[..., MLA_NOPE:]
    q_pe = apply_axial_rope(q_pe, rope)
    k_pe = apply_axial_rope(k_pe[:, :, None, :], rope)
    q = jnp.concatenate([q_nope, q_pe], axis=-1)
    k = jnp.concatenate([k_nope, jnp.broadcast_to(k_pe, (B, T, MLA_HEADS, MLA_ROPE))], axis=-1)
    return q_heads(q, MLA_HEADS), kv_heads(k), kv_heads(v)


def diff_proj(h, rope):
    B, T, _ = h.shape
    q, k, v = jnp.split(h, [2 * DIFF_HEADS * DIFF_QK, 4 * DIFF_HEADS * DIFF_QK], axis=-1)
    q = apply_axial_rope(q.reshape(B, T, 2 * DIFF_HEADS, DIFF_QK), rope).reshape(B, T, DIFF_HEADS, 2, DIFF_QK)
    k = apply_axial_rope(k.reshape(B, T, 2 * DIFF_HEADS, DIFF_QK), rope).reshape(B, T, DIFF_HEADS, 2, DIFF_QK)
    v = v.reshape(B, T, DIFF_HEADS, DIFF_V)
    return (q_heads(q[..., 0, :], DIFF_HEADS), q_heads(q[..., 1, :], DIFF_HEADS),
            kv_heads(k[..., 0, :]), kv_heads(k[..., 1, :]), kv_heads(v))


def sweep_query_blocks(block_fn, *qs):
    S = qs[0].shape[-2]
    nb = S // QBLOCK

    def split(a):
        return jnp.moveaxis(a.reshape(a.shape[:-2] + (nb, QBLOCK, a.shape[-1])), -3, 0)

    out = lax.map(lambda xs: block_fn(xs[0], *xs[1]), (jnp.arange(nb), tuple(split(a) for a in qs)))
    out = jnp.moveaxis(out, 0, -3)
    return out.reshape(out.shape[:-3] + (S, out.shape[-1]))


def _scores(q, k, scale):
    return jnp.einsum('bkgqd,bktd->bkgqt', q, k, preferred_element_type=jnp.float32) * scale


def _weigh(p, v):
    return jnp.einsum('bkgqt,bktd->bkgqd', p.astype(v.dtype), v)


def _sink_column(sink, s):
    return jnp.broadcast_to(sink.astype(jnp.float32)[None, :, :, None, None], s.shape[:-1] + (1,))


def dense_attention(q, k, v, scale, sink=None):
    def block(b, qb):
        s = _scores(qb, k, scale)
        if sink is None:
            return _weigh(jax.nn.softmax(s, axis=-1), v)
        p = jax.nn.softmax(jnp.concatenate([s, _sink_column(sink, s)], axis=-1), axis=-1)[..., :-1]
        return _weigh(p, v)
    return sweep_query_blocks(block, q)


def diff_attention(q1, q2, k1, k2, v, lam, scale):
    def block(b, q1b, q2b):
        p = (jax.nn.softmax(_scores(q1b, k1, scale), axis=-1)
             - lam * jax.nn.softmax(_scores(q2b, k2, scale), axis=-1))
        return _weigh(p, v)
    return sweep_query_blocks(block, q1, q2)


def window_attention(q, k, v, k_ctx, v_ctx, sink, scale):
    S = q.shape[-2]
    pad = ((0, 0), (0, 0), (QBLOCK, QBLOCK), (0, 0))
    k_pad, v_pad = jnp.pad(k, pad), jnp.pad(v, pad)
    qi = jnp.arange(QBLOCK)[:, None]
    kj = jnp.arange(3 * QBLOCK)[None, :]
    band = jnp.abs(kj - qi - QBLOCK) <= WINDOW

    def block(b, qb):
        kb = lax.dynamic_slice_in_dim(k_pad, b * QBLOCK, 3 * QBLOCK, axis=2)
        vb = lax.dynamic_slice_in_dim(v_pad, b * QBLOCK, 3 * QBLOCK, axis=2)
        j = (b - 1) * QBLOCK + kj
        allowed = band & (j >= 0) & (j < S)
        s_lat = jnp.where(allowed, _scores(qb, kb, scale), NEG_INF)
        s_ctx = _scores(qb, k_ctx, scale)
        s = jnp.concatenate([s_ctx, s_lat, _sink_column(sink, s_lat)], axis=-1)
        p = jax.nn.softmax(s, axis=-1)[..., :-1]
        return _weigh(p, jnp.concatenate([v_ctx, vb], axis=2))
    return sweep_query_blocks(block, q)


def diff_finish(o, gain, lam_init):
    return merge_heads(rms_norm(o, gain) * (1 - lam_init))


def diff_lambda_init(layer):
    return 0.8 - 0.6 * math.exp(-0.3 * layer)


def merge_branches(h_gate, outs, w_branch, w_out):
    g = jax.nn.sigmoid(h_gate.reshape(h_gate.shape[:-1] + (N_BRANCH, D_MODEL)))
    y = g[..., 0, :] * (outs[0] @ w_branch[0])
    for i in range(1, N_BRANCH):
        y = y + g[..., i, :] * (outs[i] @ w_branch[i])
    return y @ w_out


def token_mixers(u, u_c, w_in, gqa_q_norm, gqa_k_norm, mla_q_norm, mla_kv_norm, mla_w_uq, mla_w_ukv,
                 diff_lambda, diff_subln, swa_sink, w_branch, w_out, lam_init, ropes, need_ctx):
    h = jnp.split(u @ w_in, IN_SPLITS, axis=-1)
    hc = jnp.split(u_c @ w_in, IN_SPLITS, axis=-1)
    sink = swa_sink.reshape(SWA_KV_HEADS, SWA_HEADS // SWA_KV_HEADS)
    lf = diff_lambda.astype(jnp.float32)
    lam = jnp.exp(jnp.sum(lf[0] * lf[1])) - jnp.exp(jnp.sum(lf[2] * lf[3])) + lam_init

    def cat(a_ctx, a_lat):
        return jnp.concatenate([a_ctx, a_lat], axis=2)

    qa, ka, va = gqa_proj(h[0], GQA_HEADS, GQA_KV_HEADS, ropes[HEAD_DIM], gqa_q_norm, gqa_k_norm)
    qa_c, ka_c, va_c = gqa_proj(hc[0], GQA_HEADS, GQA_KV_HEADS, None, gqa_q_norm, gqa_k_norm)
    qm, km, vm = mla_proj(h[1], mla_q_norm, mla_kv_norm, mla_w_uq, mla_w_ukv, ropes[MLA_ROPE])
    qm_c, km_c, vm_c = mla_proj(hc[1], mla_q_norm, mla_kv_norm, mla_w_uq, mla_w_ukv, None)
    q1, q2, k1, k2, vd = diff_proj(h[2], ropes[DIFF_QK])
    q1_c, q2_c, k1_c, k2_c, vd_c = diff_proj(hc[2], None)
    qs, ks, vs = gqa_proj(h[3], SWA_HEADS, SWA_KV_HEADS, ropes[HEAD_DIM])
    qs_c, ks_c, vs_c = gqa_proj(hc[3], SWA_HEADS, SWA_KV_HEADS, None)

    outs = [merge_heads(dense_attention(qa, cat(ka_c, ka), cat(va_c, va), GQA_SCALE)),
            merge_heads(dense_attention(qm, cat(km_c, km), cat(vm_c, vm), MLA_SCALE)),
            diff_finish(diff_attention(q1, q2, cat(k1_c, k1), cat(k2_c, k2), cat(vd_c, vd), lam, DIFF_SCALE),
                        diff_subln, lam_init),
            merge_heads(window_attention(qs, ks, vs, ks_c, vs_c, sink, GQA_SCALE))]
    y = merge_branches(h[4], outs, w_branch, w_out)
    if not need_ctx:
        return y, None
    outs_c = [merge_heads(dense_attention(qa_c, ka_c, va_c, GQA_SCALE)),
              merge_heads(dense_attention(qm_c, km_c, vm_c, MLA_SCALE)),
              diff_finish(diff_attention(q1_c, q2_c, k1_c, k2_c, vd_c, lam, DIFF_SCALE), diff_subln, lam_init),
              merge_heads(dense_attention(qs_c, ks_c, vs_c, GQA_SCALE, sink))]
    return y, merge_branches(hc[4], outs_c, w_branch, w_out)


def setup_inputs(seed: int = 0) -> dict:
    key = jax.random.key(seed)
    ks = jax.random.split(key, 23)

    def nrm(k, shape, s):
        return s * jax.random.normal(k, shape, jnp.float32)

    def gain(k, shape):
        return 1.0 + 0.05 * jax.random.normal(k, shape, jnp.float32)

    D = D_MODEL
    return {
        'x': nrm(ks[0], (BATCH, SEQ, D), 1.0),
        'c': nrm(ks[1], (BATCH, D), 1.0),
        'ctx': nrm(ks[2], (BATCH, CTX_LEN, D), 1.0),
        'c_ctx': nrm(ks[3], (D,), 1.0),
        'w_mod': nrm(ks[4], (DEPTH, D, N_SUBLAYERS * 3 * D), 0.5 * D ** -0.5),
        'b_mod': nrm(ks[5], (DEPTH, N_SUBLAYERS * 3 * D), 0.02),
        'g_pre': gain(ks[6], (DEPTH, N_SUBLAYERS, D)),
        'g_post': gain(ks[7], (DEPTH, N_SUBLAYERS, D)),
        'w_ffn_gate': nrm(ks[8], (DEPTH, 2, D, D_FF), D ** -0.5),
        'w_ffn_up': nrm(ks[9], (DEPTH, 2, D, D_FF), D ** -0.5),
        'w_ffn_down': nrm(ks[10], (DEPTH, 2, D_FF, D), D_FF ** -0.5),
        'w_in': nrm(ks[11], (DEPTH, D, IN_COLS), D ** -0.5),
        'gqa_q_norm': gain(ks[12], (DEPTH, HEAD_DIM)),
        'gqa_k_norm': gain(ks[13], (DEPTH, HEAD_DIM)),
        'mla_q_norm': gain(ks[14], (DEPTH, MLA_Q_RANK)),
        'mla_kv_norm': gain(ks[15], (DEPTH, MLA_KV_RANK)),
        'mla_w_uq': nrm(ks[16], (DEPTH, MLA_Q_RANK, MLA_HEADS * (MLA_NOPE + MLA_ROPE)), MLA_Q_RANK ** -0.5),
        'mla_w_ukv': nrm(ks[17], (DEPTH, MLA_KV_RANK, MLA_HEADS * (MLA_NOPE + MLA_V)), MLA_KV_RANK ** -0.5),
        'diff_lambda': nrm(ks[18], (DEPTH, 4, DIFF_QK), 0.1),
        'diff_subln': gain(ks[19], (DEPTH, DIFF_V)),
        'swa_sink': nrm(ks[20], (DEPTH, SWA_HEADS), 0.5),
        'w_branch': nrm(ks[21], (DEPTH, N_BRANCH, BRANCH_W, D), BRANCH_W ** -0.5),
        'w_out': nrm(ks[22], (DEPTH, D, D), D ** -0.5),
    }


def reference(x, c, ctx, c_ctx, w_mod, b_mod, g_pre, g_post, w_ffn_gate, w_ffn_up, w_ffn_down, w_in,
              gqa_q_norm, gqa_k_norm, mla_q_norm, mla_kv_norm, mla_w_uq, mla_w_ukv,
              diff_lambda, diff_subln, swa_sink, w_branch, w_out):
    S = x.shape[1]
    ropes = {d: axial_rope_tables(S, d) for d in (HEAD_DIM, MLA_ROPE, DIFF_QK)}
    silu_c = jax.nn.silu(c)
    silu_cc = jax.nn.silu(c_ctx)[None, :]
    for l in range(DEPTH):
        need_ctx = l < DEPTH - 1
        mod = (silu_c @ w_mod[l] + b_mod[l]).reshape(-1, N_SUBLAYERS, 3, D_MODEL)
        mod_c = (silu_cc @ w_mod[l] + b_mod[l]).reshape(1, N_SUBLAYERS, 3, D_MODEL)
        ffn1 = (w_ffn_gate[l, 0], w_ffn_up[l, 0], w_ffn_down[l, 0])
        ffn2 = (w_ffn_gate[l, 1], w_ffn_up[l, 1], w_ffn_down[l, 1])
        x = ffn_half(x, mod[:, 0], g_pre[l, 0], g_post[l, 0], *ffn1)
        ctx = ffn_half(ctx, mod_c[:, 0], g_pre[l, 0], g_post[l, 0], *ffn1)
        u = adaln(x, g_pre[l, 1], mod[:, 1, 0], mod[:, 1, 1])
        u_c = adaln(ctx, g_pre[l, 1], mod_c[:, 1, 0], mod_c[:, 1, 1])
        y, y_c = token_mixers(u, u_c, w_in[l], gqa_q_norm[l], gqa_k_norm[l], mla_q_norm[l], mla_kv_norm[l],
                              mla_w_uq[l], mla_w_ukv[l], diff_lambda[l], diff_subln[l], swa_sink[l],
                              w_branch[l], w_out[l], diff_lambda_init(l), ropes, need_ctx)
        x = x + mod[:, 1, 2][:, None, :] * rms_norm(y, g_post[l, 1])
        x = ffn_half(x, mod[:, 2], g_pre[l, 2], g_post[l, 2], *ffn2)
        if need_ctx:
            ctx = ctx + mod_c[:, 1, 2][:, None, :] * rms_norm(y_c, g_post[l, 1])
            ctx = ffn_half(ctx, mod_c[:, 2], g_pre[l, 2], g_post[l, 2], *ffn2)
    return x
```

```python
import functools
import math

import jax
import jax.numpy as jnp
from jax import lax
from jax.experimental import pallas as pl
from jax.experimental.pallas import tpu as pltpu

MXU_DTYPE = jnp.bfloat16
F32 = jnp.float32

D_MODEL = 1024
D_FF = 2816
DEPTH = 2
GRID_W = 64
WINDOW = 128
ROPE_THETA = 10000.0
EPS = 1e-6
NEG_INF = -1e30
HEAD_DIM = 64
MLA_HEADS = 4
MLA_Q_RANK = 256
MLA_KV_RANK = 128
MLA_NOPE = 64
MLA_ROPE = 32
DIFF_QK = 32
N_BRANCH = 4
BRANCH_W = 256
GQA_SCALE = HEAD_DIM ** -0.5
MLA_SCALE = (MLA_NOPE + MLA_ROPE) ** -0.5
DIFF_SCALE = DIFF_QK ** -0.5

A0, M0, X0, S0, G0 = 0, 512, 928, 1696, 2208
GATE_COLS = N_BRANCH * D_MODEL
TOK_COLS = 768
FEAT_ROWS = 1664

TM = 256
MOD_ROWS = 16
MOD_TN = 1024
V7X_VMEM_LIMIT = 56 * 1024 * 1024


def _cparams(n_axes):
    return pltpu.CompilerParams(dimension_semantics=("arbitrary",) * n_axes,
                                vmem_limit_bytes=V7X_VMEM_LIMIT)


def _const_spec(shape):
    n = len(shape)
    return pl.BlockSpec(shape, lambda *_: (0,) * n)


def _sigmoid(v):
    return 1.0 / (1.0 + jnp.exp(-v))


def _dot(a, b):
    return jnp.dot(a, b, preferred_element_type=F32)


def _rms_scale(v, axis):
    return lax.rsqrt(jnp.mean(v * v, axis=axis, keepdims=True) + EPS)


def _mod_kernel(c_ref, w_ref, b_ref, o_ref):
    c = c_ref[...]
    s = (c * _sigmoid(c)).astype(MXU_DTYPE)
    o_ref[0] = _dot(s, w_ref[0].astype(MXU_DTYPE)) + b_ref[0]


def _modulation(cvec, w_mod, b_mod):
    depth, d, n = w_mod.shape
    return pl.pallas_call(
        _mod_kernel,
        grid=(depth, n // MOD_TN),
        in_specs=[pl.BlockSpec((MOD_ROWS, d), lambda l, j: (0, 0)),
                  pl.BlockSpec((1, d, MOD_TN), lambda l, j: (l, 0, j)),
                  pl.BlockSpec((1, 1, MOD_TN), lambda l, j: (l, 0, j))],
        out_specs=pl.BlockSpec((1, MOD_ROWS, MOD_TN), lambda l, j: (l, 0, j)),
        out_shape=jax.ShapeDtypeStruct((depth, MOD_ROWS, n), F32),
        compiler_params=_cparams(2),
        name="modulation",
    )(cvec, w_mod, b_mod.reshape(depth, 1, n))


def _adaln(x, g_row, shift, scale):
    return (x * _rms_scale(x, -1) * g_row) * (1.0 + scale) + shift


def _ffn_kernel(x_ref, mod_ref, gpre_ref, gpost_ref, wg_ref, wu_ref, wd_ref, o_ref, *, sub):
    x = x_ref[0]
    shift = mod_ref[0, 0, 3 * sub:3 * sub + 1, :]
    scale = mod_ref[0, 0, 3 * sub + 1:3 * sub + 2, :]
    gate = mod_ref[0, 0, 3 * sub + 2:3 * sub + 3, :]
    u = _adaln(x, gpre_ref[...], shift, scale).astype(MXU_DTYPE)
    a = _dot(u, wg_ref[...])
    b = _dot(u, wu_ref[...])
    h = ((a * _sigmoid(a)) * b).astype(MXU_DTYPE)
    y = _dot(h, wd_ref[...])
    yn = y * _rms_scale(y, -1) * gpost_ref[...]
    o_ref[0] = x + 0.5 * gate * yn


def _token_specs(n_ctx_tiles):
    x_spec = pl.BlockSpec((1, TM, D_MODEL), lambda b, t: (b, t, 0))
    mod_spec = pl.BlockSpec((1, 1, 9, D_MODEL),
                            lambda b, t: (b, jnp.where(t < n_ctx_tiles, 0, 1), 0, 0))
    return x_spec, mod_spec


def _ffn_half(xs, modsel, g_pre, g_post, wg, wu, wd, *, sub, n_ctx_tiles):
    bsz, t_all, d = xs.shape
    x_spec, mod_spec = _token_specs(n_ctx_tiles)
    return pl.pallas_call(
        functools.partial(_ffn_kernel, sub=sub),
        grid=(bsz, t_all // TM),
        in_specs=[x_spec, mod_spec, _const_spec((1, d)), _const_spec((1, d)),
                  _const_spec(wg.shape), _const_spec(wu.shape), _const_spec(wd.shape)],
        out_specs=x_spec,
        out_shape=jax.ShapeDtypeStruct(xs.shape, F32),
        compiler_params=_cparams(2),
        name=f"ffn_half_{sub}",
    )(xs, modsel, g_pre, g_post, wg, wu, wd)


def _rope_lanes(v, cos, sin, pair):
    width = v.shape[1]
    lane = lax.broadcasted_iota(jnp.int32, v.shape, 1)
    first = (lane & pair) == 0
    partner = jnp.where(first, pltpu.roll(v, width - pair, 1), pltpu.roll(v, pair, 1))
    return v * cos + partner * sin


def _swap_row_blocks(v, pair):
    parts = []
    for r in range(0, v.shape[0], 2 * pair):
        parts.append(v[r + pair:r + 2 * pair])
        parts.append(v[r:r + pair])
    return jnp.concatenate(parts, axis=0)


def _rope_rows(v, cos, sin, pair):
    return v * cos + _swap_row_blocks(v, pair) * sin


def _inproj_kernel(x_ref, mod_ref, gpre_ref, wtok_ref, wft_ref, wuq_ref, wkx_ref, wv_ref,
                   gk_ref, gkv_row_ref, gq_col_ref, gmq_col_ref, gkv_col_ref,
                   rope_t_ref, rope_f_ref,
                   ka_ref, km_ref, kd_ref, ks_ref, g_ref,
                   qa_ref, vta_ref, qm_ref, vtm_ref, qd_ref, vtd_ref, qs_ref, vts_ref):
    x = x_ref[0]
    shift = mod_ref[0, 0, 3:4, :]
    scale = mod_ref[0, 0, 4:5, :]
    u = _adaln(x, gpre_ref[...], shift, scale).astype(MXU_DTYPE)

    c64, s64, c32, s32 = rope_t_ref[0], rope_t_ref[1], rope_t_ref[2], rope_t_ref[3]

    ht = _dot(u, wtok_ref[:, 0:TOK_COLS])
    k_a = ht[:, 0:128]
    sq = k_a * k_a
    lane = lax.broadcasted_iota(jnp.int32, sq.shape, 1)
    low = lane < HEAD_DIM
    ss_lo = jnp.sum(jnp.where(low, sq, 0.0), axis=1, keepdims=True)
    ss_hi = jnp.sum(jnp.where(low, 0.0, sq), axis=1, keepdims=True)
    ms = jnp.where(low, ss_lo, ss_hi) * (1.0 / HEAD_DIM)
    k_a = k_a * lax.rsqrt(ms + EPS) * gk_ref[...]
    ka_ref[0] = _rope_lanes(k_a, c64, s64, 16).astype(MXU_DTYPE)

    ckv = ht[:, 128:256]
    ckv_n = (ckv * _rms_scale(ckv, -1) * gkv_row_ref[...]).astype(MXU_DTYPE)
    k_nope = _dot(ckv_n, wkx_ref[...])
    k_pe = _rope_lanes(ht[:, 256:384], c32, s32, 8)
    for h in range(MLA_HEADS):
        km_ref[0, :, 128 * h:128 * h + 128] = (k_nope[:, 128 * h:128 * h + 128] + k_pe).astype(MXU_DTYPE)

    for j in range(2):
        kd_ref[0, :, 128 * j:128 * j + 128] = _rope_lanes(
            ht[:, 384 + 128 * j:512 + 128 * j], c32, s32, 8).astype(MXU_DTYPE)
    ks_ref[0] = _rope_lanes(ht[:, 640:768], c64, s64, 16).astype(MXU_DTYPE)

    for j in range(GATE_COLS // D_MODEL):
        lo = TOK_COLS + j * D_MODEL
        hg = _dot(u, wtok_ref[:, lo:lo + D_MODEL])
        g_ref[0, :, j * D_MODEL:(j + 1) * D_MODEL] = _sigmoid(hg).astype(g_ref.dtype)

    hf = lax.dot_general(wft_ref[...], u, (((1,), (1,)), ((), ())), preferred_element_type=F32)
    fc64, fs64 = rope_f_ref[0:64, :], rope_f_ref[64:128, :]
    fc32, fs32 = rope_f_ref[128:160, :], rope_f_ref[160:192, :]
    fcm, fsm = rope_f_ref[192:320, :], rope_f_ref[320:448, :]
    zeros64 = jnp.zeros((HEAD_DIM, TM), MXU_DTYPE)

    def place_heads(ref, heads):
        for h, q in enumerate(heads):
            r0 = HEAD_DIM * (h // 2)
            ref[0, 0, r0:r0 + HEAD_DIM, h * TM:(h + 1) * TM] = q
            z0 = HEAD_DIM - r0
            ref[0, 0, z0:z0 + HEAD_DIM, h * TM:(h + 1) * TM] = zeros64

    q_heads = []
    for h in range(4):
        q = hf[64 * h:64 * h + 64]
        q = q * _rms_scale(q, 0) * gq_col_ref[...]
        q_heads.append((_rope_rows(q, fc64, fs64, 16) * GQA_SCALE).astype(MXU_DTYPE))
    place_heads(qa_ref, q_heads)
    vta_ref[0] = hf[256:384].astype(MXU_DTYPE)

    cq = hf[384:640]
    cq_n = (cq * _rms_scale(cq, 0) * gmq_col_ref[...]).astype(MXU_DTYPE)
    qm = _dot(wuq_ref[...], cq_n)
    for h in range(MLA_HEADS):
        qh = qm[128 * h:128 * h + 128]
        qm_ref[0, 128 * h:128 * h + 128, :] = (_rope_rows(qh, fcm, fsm, 8) * MLA_SCALE).astype(MXU_DTYPE)
    ckv_f = hf[640:768]
    ckv_fn = (ckv_f * _rms_scale(ckv_f, 0) * gkv_col_ref[...]).astype(MXU_DTYPE)
    vtm_ref[0] = _dot(wv_ref[...], ckv_fn).astype(MXU_DTYPE)

    zeros32 = jnp.zeros((DIFF_QK, TM), MXU_DTYPE)
    for c in range(2):
        for rb in range(4):
            r0 = 768 + 128 * c + 32 * rb
            piece = (_rope_rows(hf[r0:r0 + 32], fc32, fs32, 8) * DIFF_SCALE).astype(MXU_DTYPE)
            for cb in range(4):
                qd_ref[0, 0, c, 32 * rb:32 * rb + 32, cb * TM:(cb + 1) * TM] = piece if cb == rb else zeros32
    vtd_ref[0] = hf[1024:1280].astype(MXU_DTYPE)

    s_heads = []
    for h in range(4):
        q = hf[1280 + 64 * h:1344 + 64 * h]
        s_heads.append((_rope_rows(q, fc64, fs64, 16) * GQA_SCALE).astype(MXU_DTYPE))
    place_heads(qs_ref, s_heads)
    vts_ref[0] = hf[1536:1664].astype(MXU_DTYPE)


def _inproj(xs, modsel, g_pre, w, rope_t, rope_f, *, n_ctx_tiles):
    bsz, t_all, d = xs.shape
    nt = t_all // TM
    bt = lambda t, b: (b, t, 0)
    x_spec = pl.BlockSpec((1, TM, d), bt)
    mod_spec = pl.BlockSpec((1, 1, 9, d), lambda t, b: (b, jnp.where(t < n_ctx_tiles, 0, 1), 0, 0))
    tok = lambda n: (pl.BlockSpec((1, TM, n), bt), jax.ShapeDtypeStruct((bsz, t_all, n), MXU_DTYPE))
    feat = lambda n: (pl.BlockSpec((1, n, TM), lambda t, b: (b, 0, t)),
                      jax.ShapeDtypeStruct((bsz, n, t_all), MXU_DTYPE))
    qblk = (pl.BlockSpec((1, 1, 128, 4 * TM), lambda t, b: (b, t, 0, 0)),
            jax.ShapeDtypeStruct((bsz, nt, 128, 4 * TM), MXU_DTYPE))
    qdblk = (pl.BlockSpec((1, 1, 2, 128, 4 * TM), lambda t, b: (b, t, 0, 0, 0)),
             jax.ShapeDtypeStruct((bsz, nt, 2, 128, 4 * TM), MXU_DTYPE))
    outs = [tok(128), tok(512), tok(256), tok(128), tok(GATE_COLS),
            qblk, feat(128), feat(512), feat(256), qdblk, feat(256), qblk, feat(128)]
    consts = [g_pre, w["w_tok"], w["w_feat_t"], w["w_uq_t"], w["w_k_exp"], w["w_v_t"],
              w["gk_row"], w["gkv_row"], w["gq_col"], w["gmq_col"], w["gkv_col"]]
    return pl.pallas_call(
        _inproj_kernel,
        grid=(nt, bsz),
        in_specs=[x_spec, mod_spec] + [_const_spec(c.shape) for c in consts]
        + [pl.BlockSpec((4, TM, 128), lambda t, b: (0, t, 0)),
           pl.BlockSpec((448, TM), lambda t, b: (0, t))],
        out_specs=[o[0] for o in outs],
        out_shape=[o[1] for o in outs],
        compiler_params=_cparams(2),
        name="mixer_inproj",
    )(xs, modsel, *consts, rope_t, rope_f)


def _flash_update(s, vt, m_ref, l_ref, acc_ref, u, blocks):
    m_old = m_ref[u]
    m_new = jnp.maximum(m_old, jnp.max(s, axis=0, keepdims=True))
    alpha = jnp.exp(m_old - m_new)
    p = jnp.exp(s - m_new)
    l_ref[u] = alpha * l_ref[u] + jnp.sum(p, axis=0, keepdims=True)
    m_ref[u] = m_new
    pb = p.astype(MXU_DTYPE)
    for c0, r0 in blocks:
        pv = _dot(vt[r0:r0 + HEAD_DIM, :], pb[:, c0:c0 + TM])
        acc_ref[u, :, c0:c0 + TM] = alpha[:, c0:c0 + TM] * acc_ref[u, :, c0:c0 + TM] + pv


def _dense_attn_kernel(*refs, units, tk, n_ctx, n_lat, diff):
    if diff:
        q_ref, k_ref, vt_ref, lam_ref, subln_ref, o_ref, m_ref, l_ref, acc_ref = refs
    else:
        q_ref, k_ref, vt_ref, o_ref, m_ref, l_ref, acc_ref = refs
    qt = pl.program_id(1)
    m_ref[...] = jnp.full(m_ref.shape, NEG_INF, F32)
    l_ref[...] = jnp.zeros(l_ref.shape, F32)
    acc_ref[...] = jnp.zeros(acc_ref.shape, F32)

    def visit(off, size):
        for u, (q_idx, k0, blocks) in enumerate(units):
            k = k_ref[0, pl.ds(off, size), k0:k0 + 128]
            vt = vt_ref[0, :, pl.ds(off, size)]
            s = _dot(k, q_ref[q_idx])
            _flash_update(s, vt, m_ref, l_ref, acc_ref, u, blocks)

    visit(0, n_ctx)

    def body(i, carry):
        visit(pl.multiple_of(n_ctx + i * tk, 128), tk)
        return carry

    lax.fori_loop(0, jnp.where(qt == 0, 0, n_lat // tk), body, 0)

    if not diff:
        for u, (_, _, blocks) in enumerate(units):
            inv = 1.0 / l_ref[u]
            for c0, r0 in blocks:
                h = (u * len(blocks) * TM + c0) // TM
                o_ref[0, HEAD_DIM * h:HEAD_DIM * h + HEAD_DIM, :] = acc_ref[u, :, c0:c0 + TM] * inv[:, c0:c0 + TM]
    else:
        lf = lam_ref[...]
        lam_init = lam_ref[4:5, 0:1]
        lam = (jnp.exp(jnp.sum(lf[0:1, 0:DIFF_QK] * lf[1:2, 0:DIFF_QK], axis=1, keepdims=True))
               - jnp.exp(jnp.sum(lf[2:3, 0:DIFF_QK] * lf[3:4, 0:DIFF_QK], axis=1, keepdims=True)) + lam_init)
        for u in range(len(units)):
            inv = 1.0 / l_ref[u]
            for hh in range(2):
                c1, c2 = 2 * hh * TM, (2 * hh + 1) * TM
                o = (acc_ref[u, :, c1:c1 + TM] * inv[:, c1:c1 + TM]
                     - lam * (acc_ref[u, :, c2:c2 + TM] * inv[:, c2:c2 + TM]))
                o = o * _rms_scale(o, 0) * subln_ref[...] * (1.0 - lam_init)
                h = 2 * u + hh
                o_ref[0, HEAD_DIM * h:HEAD_DIM * h + HEAD_DIM, :] = o


def _dense_attention(q, k, vt, *, kind, n_ctx, tk, lam_tab=None, subln_col=None):
    bsz, t_all, dk = k.shape
    nt = t_all // TM
    dv = vt.shape[1]
    n_lat = t_all - n_ctx
    if kind == "gqa":
        q_spec = pl.BlockSpec((1, 1, 128, 4 * TM), lambda b, t: (b, t, 0, 0))
        units = (((0, 0), 0, tuple((h * TM, HEAD_DIM * (h // 2)) for h in range(4))),)
        ncols = 4 * TM
    elif kind == "diff":
        q_spec = pl.BlockSpec((1, 1, 2, 128, 4 * TM), lambda b, t: (b, t, 0, 0, 0))
        units = tuple(((0, 0, u), 128 * u,
                       tuple(((2 * hh + mm) * TM, HEAD_DIM * (2 * u + hh)) for hh in range(2) for mm in range(2)))
                      for u in range(2))
        ncols = 4 * TM
    else:
        q_spec = pl.BlockSpec((1, 512, TM), lambda b, t: (b, 0, t))
        units = tuple(((0, slice(128 * h, 128 * h + 128)), 128 * h, ((0, HEAD_DIM * h),)) for h in range(4))
        ncols = TM
    in_specs = [q_spec,
                pl.BlockSpec((1, t_all, dk), lambda b, t: (b, 0, 0)),
                pl.BlockSpec((1, dv, t_all), lambda b, t: (b, 0, 0))]
    args = [q, k, vt]
    if kind == "diff":
        in_specs += [_const_spec(lam_tab.shape), _const_spec(subln_col.shape)]
        args += [lam_tab, subln_col]
    nu = len(units)
    return pl.pallas_call(
        functools.partial(_dense_attn_kernel, units=units, tk=tk, n_ctx=n_ctx, n_lat=n_lat,
                          diff=(kind == "diff")),
        grid=(bsz, nt),
        in_specs=in_specs,
        out_specs=pl.BlockSpec((1, BRANCH_W, TM), lambda b, t: (b, 0, t)),
        out_shape=jax.ShapeDtypeStruct((bsz, BRANCH_W, t_all), F32),
        scratch_shapes=[pltpu.VMEM((nu, 1, ncols), F32), pltpu.VMEM((nu, 1, ncols), F32),
                        pltpu.VMEM((nu, HEAD_DIM, ncols), F32)],
        compiler_params=_cparams(2),
        name=f"attn_{kind}",
    )(*args)


def _window_attn_kernel(q_ref, k_ref, vt_ref, sink_ref, o_ref, m_ref, l_ref, acc_ref, *, n_ctx, t_all, wk):
    qt = pl.program_id(1)
    blocks = tuple((h * TM, HEAD_DIM * (h // 2)) for h in range(4))
    m_ref[0] = sink_ref[...]
    l_ref[0] = jnp.ones(l_ref.shape[1:], F32)
    acc_ref[...] = jnp.zeros(acc_ref.shape, F32)
    q = q_ref[0, 0]

    s = _dot(k_ref[0, 0:n_ctx, :], q)
    _flash_update(s, vt_ref[0, :, 0:n_ctx], m_ref, l_ref, acc_ref, 0, blocks)

    q0 = qt * TM
    start = jnp.clip(q0 - WINDOW, n_ctx, t_all - wk)
    start = pl.multiple_of(start, 128)
    s = _dot(k_ref[0, pl.ds(start, wk), :], q)
    kpos = start + lax.broadcasted_iota(jnp.int32, s.shape, 0)
    qpos = q0 + (lax.broadcasted_iota(jnp.int32, s.shape, 1) & (TM - 1))
    allowed = (jnp.abs(kpos - qpos) <= WINDOW) & (qpos >= n_ctx)
    s = jnp.where(allowed, s, NEG_INF)
    _flash_update(s, vt_ref[0, :, pl.ds(start, wk)], m_ref, l_ref, acc_ref, 0, blocks)

    inv = 1.0 / l_ref[0]
    for h, (c0, _) in enumerate(blocks):
        o_ref[0, HEAD_DIM * h:HEAD_DIM * h + HEAD_DIM, :] = acc_ref[0, :, c0:c0 + TM] * inv[:, c0:c0 + TM]


def _window_attention(q, k, vt, sink_row, *, n_ctx):
    bsz, t_all, dk = k.shape
    nt = t_all // TM
    wk = TM + 2 * WINDOW
    return pl.pallas_call(
        functools.partial(_window_attn_kernel, n_ctx=n_ctx, t_all=t_all, wk=wk),
        grid=(bsz, nt),
        in_specs=[pl.BlockSpec((1, 1, 128, 4 * TM), lambda b, t: (b, t, 0, 0)),
                  pl.BlockSpec((1, t_all, dk), lambda b, t: (b, 0, 0)),
                  pl.BlockSpec((1, vt.shape[1], t_all), lambda b, t: (b, 0, 0)),
                  _const_spec(sink_row.shape)],
        out_specs=pl.BlockSpec((1, BRANCH_W, TM), lambda b, t: (b, 0, t)),
        out_shape=jax.ShapeDtypeStruct((bsz, BRANCH_W, t_all), F32),
        scratch_shapes=[pltpu.VMEM((1, 1, 4 * TM), F32), pltpu.VMEM((1, 1, 4 * TM), F32),
                        pltpu.VMEM((1, HEAD_DIM, 4 * TM), F32)],
        compiler_params=_cparams(2),
        name="attn_window",
    )(q, k, vt, sink_row)


def _merge_kernel(x_ref, mod_ref, gpost_ref, oa_ref, om_ref, od_ref, os_ref, g_ref, wb_ref, wo_ref, o_ref):
    x = x_ref[0]
    gate = mod_ref[0, 0, 5:6, :]
    y = None
    for i, o_t in enumerate((oa_ref, om_ref, od_ref, os_ref)):
        o_tok = o_t[0].T.astype(MXU_DTYPE)
        term = g_ref[0, :, i * D_MODEL:(i + 1) * D_MODEL].astype(F32) * _dot(o_tok, wb_ref[i])
        y = term if y is None else y + term
    z = _dot(y.astype(MXU_DTYPE), wo_ref[...])
    o_ref[0] = x + gate * (z * _rms_scale(z, -1) * gpost_ref[...])


def _merge(xs, modsel, g_post, outs, gates, w_branch, w_out, *, n_ctx_tiles):
    bsz, t_all, d = xs.shape
    x_spec, mod_spec = _token_specs(n_ctx_tiles)
    o_spec = pl.BlockSpec((1, BRANCH_W, TM), lambda b, t: (b, 0, t))
    return pl.pallas_call(
        _merge_kernel,
        grid=(bsz, t_all // TM),
        in_specs=[x_spec, mod_spec, _const_spec((1, d)), o_spec, o_spec, o_spec, o_spec,
                  pl.BlockSpec((1, TM, GATE_COLS), lambda b, t: (b, t, 0)),
                  _const_spec(w_branch.shape), _const_spec(w_out.shape)],
        out_specs=x_spec,
        out_shape=jax.ShapeDtypeStruct(xs.shape, F32),
        compiler_params=_cparams(2),
        name="branch_merge",
    )(xs, modsel, g_post, *outs, gates, w_branch, w_out)


def _rope_tables(n_ctx, n_lat):
    pos = jnp.arange(n_lat)
    row = (pos // GRID_W).astype(F32)
    col = (pos % GRID_W).astype(F32)

    def lane_pattern(rot_dim):
        half = rot_dim // 2
        inv_freq = ROPE_THETA ** (-jnp.arange(0, half, 2, dtype=F32) / half)
        ar, ac = row[:, None] * inv_freq[None, :], col[:, None] * inv_freq[None, :]
        cos = jnp.concatenate([jnp.cos(ar), jnp.cos(ar), jnp.cos(ac), jnp.cos(ac)], axis=1)
        sin = jnp.concatenate([-jnp.sin(ar), jnp.sin(ar), -jnp.sin(ac), jnp.sin(ac)], axis=1)
        cos = jnp.concatenate([jnp.ones((n_ctx, rot_dim), F32), cos], axis=0)
        sin = jnp.concatenate([jnp.zeros((n_ctx, rot_dim), F32), sin], axis=0)
        return cos, sin

    c64, s64 = lane_pattern(HEAD_DIM)
    c32, s32 = lane_pattern(DIFF_QK)
    t_all = n_ctx + n_lat
    rope_t = jnp.stack([jnp.tile(c64, (1, 2)), jnp.tile(s64, (1, 2)), jnp.tile(c32, (1, 4)), jnp.tile(s32, (1, 4))])
    ones, zeros = jnp.ones((MLA_NOPE, t_all), F32), jnp.zeros((MLA_NOPE, t_all), F32)
    pad1, pad0 = jnp.ones((32, t_all), F32), jnp.zeros((32, t_all), F32)
    rope_f = jnp.concatenate([c64.T, s64.T, c32.T, s32.T,
                              ones, c32.T, pad1, zeros, s32.T, pad0], axis=0)
    return rope_t, rope_f


def _layer_weights(l, w_in, gqa_q_norm, gqa_k_norm, mla_q_norm, mla_kv_norm, mla_w_uq, mla_w_ukv):
    w = w_in[l]
    col = lambda a, n: w[:, a:a + n]
    kpe = jnp.zeros((D_MODEL, 128), F32).at[:, 64:96].set(col(M0 + 384, 32))
    w_tok = jnp.concatenate([col(A0 + 256, 128), col(M0 + 256, 128), kpe, col(X0 + 256, 256),
                             col(S0 + 256, 128), col(G0, GATE_COLS)], axis=1)
    w_feat = jnp.concatenate([col(A0, 256), col(A0 + 384, 128), col(M0, 256), col(M0 + 256, 128),
                              col(X0, 256), col(X0 + 512, 256), col(S0, 256), col(S0 + 384, 128)], axis=1)
    uq = mla_w_uq[l].reshape(MLA_Q_RANK, MLA_HEADS, MLA_NOPE + MLA_ROPE)
    uq = jnp.pad(uq, ((0, 0), (0, 0), (0, 128 - MLA_NOPE - MLA_ROPE))).reshape(MLA_Q_RANK, 512)
    ukv = mla_w_ukv[l].reshape(MLA_KV_RANK, MLA_HEADS, 2 * MLA_NOPE)
    k_exp = jnp.pad(ukv[:, :, :MLA_NOPE], ((0, 0), (0, 0), (0, 64))).reshape(MLA_KV_RANK, 512)
    v_t = ukv[:, :, MLA_NOPE:].reshape(MLA_KV_RANK, 256).T
    return {
        "w_tok": w_tok.astype(MXU_DTYPE), "w_feat_t": w_feat.T.astype(MXU_DTYPE),
        "w_uq_t": uq.T.astype(MXU_DTYPE), "w_k_exp": k_exp.astype(MXU_DTYPE), "w_v_t": v_t.astype(MXU_DTYPE),
        "gk_row": jnp.tile(gqa_k_norm[l], 2)[None, :], "gkv_row": mla_kv_norm[l][None, :],
        "gq_col": gqa_q_norm[l][:, None], "gmq_col": mla_q_norm[l][:, None], "gkv_col": mla_kv_norm[l][:, None],
    }


def kernel(x, c, ctx, c_ctx, w_mod, b_mod, g_pre, g_post, w_ffn_gate, w_ffn_up, w_ffn_down, w_in,
           gqa_q_norm, gqa_k_norm, mla_q_norm, mla_kv_norm, mla_w_uq, mla_w_ukv,
           diff_lambda, diff_subln, swa_sink, w_branch, w_out):
    bsz, n_lat, d = x.shape
    n_ctx = ctx.shape[1]
    assert d == D_MODEL and n_ctx == TM and n_lat % 512 == 0 and bsz + 1 <= MOD_ROWS
    n_ctx_tiles = n_ctx // TM
    tk = 1024 if n_lat % 1024 == 0 else 512

    cvec = jnp.zeros((MOD_ROWS, d), F32).at[:bsz].set(c).at[bsz].set(c_ctx)
    mod = _modulation(cvec, w_mod, b_mod).reshape(DEPTH, MOD_ROWS, 9, d)
    modsel = jnp.stack([jnp.broadcast_to(mod[:, bsz:bsz + 1], (DEPTH, bsz, 9, d)), mod[:, :bsz]], axis=2)

    rope_t, rope_f = _rope_tables(n_ctx, n_lat)
    xs = jnp.concatenate([ctx, x], axis=1)
    cast = lambda a: a.astype(MXU_DTYPE)

    for l in range(DEPTH):
        ffn = lambda v, i, sub: _ffn_half(v, modsel[l], g_pre[l, sub][None], g_post[l, sub][None],
                                          cast(w_ffn_gate[l, i]), cast(w_ffn_up[l, i]), cast(w_ffn_down[l, i]),
                                          sub=sub, n_ctx_tiles=n_ctx_tiles)
        xs = ffn(xs, 0, 0)
        w = _layer_weights(l, w_in, gqa_q_norm, gqa_k_norm, mla_q_norm, mla_kv_norm, mla_w_uq, mla_w_ukv)
        (k_a, k_m, k_d, k_s, gates, q_a, vt_a, q_m, vt_m, q_d, vt_d, q_s, vt_s) = _inproj(
            xs, modsel[l], g_pre[l, 1][None], w, rope_t, rope_f, n_ctx_tiles=n_ctx_tiles)
        lam_init = 0.8 - 0.6 * math.exp(-0.3 * l)
        lam_tab = jnp.zeros((8, 128), F32).at[0:4, 0:DIFF_QK].set(diff_lambda[l]).at[4, :].set(lam_init)
        sink_row = jnp.repeat(swa_sink[l], TM)[None, :]
        outs = (_dense_attention(q_a, k_a, vt_a, kind="gqa", n_ctx=n_ctx, tk=tk),
                _dense_attention(q_m, k_m, vt_m, kind="mla", n_ctx=n_ctx, tk=tk),
                _dense_attention(q_d, k_d, vt_d, kind="diff", n_ctx=n_ctx, tk=tk,
                                 lam_tab=lam_tab, subln_col=diff_subln[l][:, None]),
                _window_attention(q_s, k_s, vt_s, sink_row, n_ctx=n_ctx))
        xs = _merge(xs, modsel[l], g_post[l, 1][None], outs, gates, cast(w_branch[l]), cast(w_out[l]),
                    n_ctx_tiles=n_ctx_tiles)
        xs = ffn(xs, 1, 2)
    return xs[:, n_ctx:, :]
```

```python
import functools
import math

import jax
import jax.numpy as jnp
from jax import lax
from jax.experimental import pallas as pl
from jax.experimental.pallas import tpu as pltpu

MXU_DTYPE = jnp.bfloat16
F32 = jnp.float32

D_MODEL = 1024
D_FF = 2816
DEPTH = 2
GRID_W = 64
WINDOW = 128
ROPE_THETA = 10000.0
EPS = 1e-6
NEG_INF = -1e30
HEAD_DIM = 64
MLA_HEADS = 4
MLA_Q_RANK = 256
MLA_KV_RANK = 128
MLA_NOPE = 64
MLA_ROPE = 32
DIFF_QK = 32
N_BRANCH = 4
BRANCH_W = 256
GQA_SCALE = HEAD_DIM ** -0.5
MLA_SCALE = (MLA_NOPE + MLA_ROPE) ** -0.5
DIFF_SCALE = DIFF_QK ** -0.5
LOG2E = math.log2(math.e)
ONES_ROWS = 16
ACC_ROWS = HEAD_DIM + ONES_ROWS

A0, M0, X0, S0, G0 = 0, 512, 928, 1696, 2208
GATE_COLS = N_BRANCH * D_MODEL
TOK_COLS = 768
FEAT_ROWS = 1664

TM = 256
TK = 512
STAGES_PER_TRIP = 4
MOD_ROWS = 16
MOD_TN = 1024
V7X_VMEM_LIMIT = 56 * 1024 * 1024


def _cparams(n_axes):
    return pltpu.CompilerParams(dimension_semantics=("arbitrary",) * n_axes,
                                vmem_limit_bytes=V7X_VMEM_LIMIT)


def _const_spec(shape):
    n = len(shape)
    return pl.BlockSpec(shape, lambda *_: (0,) * n)


def _sigmoid(v):
    return 1.0 / (1.0 + jnp.exp(-v))


def _dot(a, b):
    return jnp.dot(a, b, preferred_element_type=F32)


def _rms_scale(v, axis):
    return lax.rsqrt(jnp.mean(v * v, axis=axis, keepdims=True) + EPS)


def _mod_kernel(c_ref, w_ref, b_ref, o_ref):
    c = c_ref[...]
    s = (c * _sigmoid(c)).astype(MXU_DTYPE)
    o_ref[0] = _dot(s, w_ref[0].astype(MXU_DTYPE)) + b_ref[0]


def _modulation(cvec, w_mod, b_mod):
    depth, d, n = w_mod.shape
    return pl.pallas_call(
        _mod_kernel,
        grid=(depth, n // MOD_TN),
        in_specs=[pl.BlockSpec((MOD_ROWS, d), lambda l, j: (0, 0)),
                  pl.BlockSpec((1, d, MOD_TN), lambda l, j: (l, 0, j)),
                  pl.BlockSpec((1, 1, MOD_TN), lambda l, j: (l, 0, j))],
        out_specs=pl.BlockSpec((1, MOD_ROWS, MOD_TN), lambda l, j: (l, 0, j)),
        out_shape=jax.ShapeDtypeStruct((depth, MOD_ROWS, n), F32),
        compiler_params=_cparams(2),
        name="modulation",
    )(cvec, w_mod, b_mod.reshape(depth, 1, n))


def _adaln(x, g_row, shift, scale):
    return (x * _rms_scale(x, -1) * g_row) * (1.0 + scale) + shift


def _ffn_kernel(x_ref, mod_ref, gpre_ref, gpost_ref, wg_ref, wu_ref, wd_ref, o_ref, *, sub):
    x = x_ref[0]
    shift = mod_ref[0, 0, 3 * sub:3 * sub + 1, :]
    scale = mod_ref[0, 0, 3 * sub + 1:3 * sub + 2, :]
    gate = mod_ref[0, 0, 3 * sub + 2:3 * sub + 3, :]
    u = _adaln(x, gpre_ref[...], shift, scale).astype(MXU_DTYPE)
    a = _dot(u, wg_ref[...])
    b = _dot(u, wu_ref[...])
    h = ((a * _sigmoid(a)) * b).astype(MXU_DTYPE)
    y = _dot(h, wd_ref[...])
    yn = y * _rms_scale(y, -1) * gpost_ref[...]
    o_ref[0] = x + 0.5 * gate * yn


def _token_specs(n_ctx_tiles):
    x_spec = pl.BlockSpec((1, TM, D_MODEL), lambda b, t: (b, t, 0))
    mod_spec = pl.BlockSpec((1, 1, 9, D_MODEL),
                            lambda b, t: (b, jnp.where(t < n_ctx_tiles, 0, 1), 0, 0))
    return x_spec, mod_spec


def _ffn_half(xs, modsel, g_pre, g_post, wg, wu, wd, *, sub, n_ctx_tiles):
    bsz, t_all, d = xs.shape
    x_spec, mod_spec = _token_specs(n_ctx_tiles)
    return pl.pallas_call(
        functools.partial(_ffn_kernel, sub=sub),
        grid=(bsz, t_all // TM),
        in_specs=[x_spec, mod_spec, _const_spec((1, d)), _const_spec((1, d)),
                  _const_spec(wg.shape), _const_spec(wu.shape), _const_spec(wd.shape)],
        out_specs=x_spec,
        out_shape=jax.ShapeDtypeStruct(xs.shape, F32),
        compiler_params=_cparams(2),
        name=f"ffn_half_{sub}",
    )(xs, modsel, g_pre, g_post, wg, wu, wd)


def _rope_lanes(v, cos, sin, pair):
    width = v.shape[1]
    lane = lax.broadcasted_iota(jnp.int32, v.shape, 1)
    first = (lane & pair) == 0
    partner = jnp.where(first, pltpu.roll(v, width - pair, 1), pltpu.roll(v, pair, 1))
    return v * cos + partner * sin


def _swap_row_blocks(v, pair):
    parts = []
    for r in range(0, v.shape[0], 2 * pair):
        parts.append(v[r + pair:r + 2 * pair])
        parts.append(v[r:r + pair])
    return jnp.concatenate(parts, axis=0)


def _rope_rows(v, cos, sin, pair):
    return v * cos + _swap_row_blocks(v, pair) * sin


def _inproj_kernel(x_ref, mod_ref, gpre_ref, wtok_ref, wft_ref, wuq_ref, wkx_ref, wv_ref,
                   gk_ref, gkv_row_ref, gq_col_ref, gmq_col_ref, gkv_col_ref,
                   rope_t_ref, rope_f_ref,
                   ka_ref, km_ref, kd_ref, ks_ref, g_ref,
                   qa_ref, vta_ref, qm_ref, vtm_ref, qd_ref, vtd_ref, qs_ref, vts_ref):
    x = x_ref[0]
    shift = mod_ref[0, 0, 3:4, :]
    scale = mod_ref[0, 0, 4:5, :]
    u = _adaln(x, gpre_ref[...], shift, scale).astype(MXU_DTYPE)

    c64, s64, c32, s32 = rope_t_ref[0], rope_t_ref[1], rope_t_ref[2], rope_t_ref[3]

    ht = _dot(u, wtok_ref[:, 0:TOK_COLS])
    k_a = ht[:, 0:128]
    sq = k_a * k_a
    lane = lax.broadcasted_iota(jnp.int32, sq.shape, 1)
    low = lane < HEAD_DIM
    ss_lo = jnp.sum(jnp.where(low, sq, 0.0), axis=1, keepdims=True)
    ss_hi = jnp.sum(jnp.where(low, 0.0, sq), axis=1, keepdims=True)
    ms = jnp.where(low, ss_lo, ss_hi) * (1.0 / HEAD_DIM)
    k_a = k_a * lax.rsqrt(ms + EPS) * gk_ref[...]
    ka_ref[0] = _rope_lanes(k_a, c64, s64, 16).astype(MXU_DTYPE)

    ckv = ht[:, 128:256]
    ckv_n = (ckv * _rms_scale(ckv, -1) * gkv_row_ref[...]).astype(MXU_DTYPE)
    k_nope = _dot(ckv_n, wkx_ref[...])
    k_pe = _rope_lanes(ht[:, 256:384], c32, s32, 8)
    for h in range(MLA_HEADS):
        km_ref[0, :, 128 * h:128 * h + 128] = (k_nope[:, 128 * h:128 * h + 128] + k_pe).astype(MXU_DTYPE)

    for j in range(2):
        kd_ref[0, :, 128 * j:128 * j + 128] = _rope_lanes(
            ht[:, 384 + 128 * j:512 + 128 * j], c32, s32, 8).astype(MXU_DTYPE)
    ks_ref[0] = _rope_lanes(ht[:, 640:768], c64, s64, 16).astype(MXU_DTYPE)

    for j in range(GATE_COLS // D_MODEL):
        lo = TOK_COLS + j * D_MODEL
        hg = _dot(u, wtok_ref[:, lo:lo + D_MODEL])
        g_ref[0, :, j * D_MODEL:(j + 1) * D_MODEL] = _sigmoid(hg).astype(g_ref.dtype)

    hf = lax.dot_general(wft_ref[...], u, (((1,), (1,)), ((), ())), preferred_element_type=F32)
    fc64, fs64 = rope_f_ref[0:64, :], rope_f_ref[64:128, :]
    fc32, fs32 = rope_f_ref[128:160, :], rope_f_ref[160:192, :]
    fcm, fsm = rope_f_ref[192:320, :], rope_f_ref[320:448, :]
    zeros64 = jnp.zeros((HEAD_DIM, TM), MXU_DTYPE)

    def place_heads(ref, heads):
        for h, q in enumerate(heads):
            r0 = HEAD_DIM * (h // 2)
            ref[0, 0, r0:r0 + HEAD_DIM, h * TM:(h + 1) * TM] = q
            z0 = HEAD_DIM - r0
            ref[0, 0, z0:z0 + HEAD_DIM, h * TM:(h + 1) * TM] = zeros64

    q_heads = []
    for h in range(4):
        q = hf[64 * h:64 * h + 64]
        q = q * _rms_scale(q, 0) * gq_col_ref[...]
        q_heads.append((_rope_rows(q, fc64, fs64, 16) * (GQA_SCALE * LOG2E)).astype(MXU_DTYPE))
    place_heads(qa_ref, q_heads)
    vta_ref[0] = hf[256:384].astype(MXU_DTYPE)

    cq = hf[384:640]
    cq_n = (cq * _rms_scale(cq, 0) * gmq_col_ref[...]).astype(MXU_DTYPE)
    qm = _dot(wuq_ref[...], cq_n)
    for h in range(MLA_HEADS):
        qh = qm[128 * h:128 * h + 128]
        qm_ref[0, 128 * h:128 * h + 128, :] = (_rope_rows(qh, fcm, fsm, 8) * (MLA_SCALE * LOG2E)).astype(MXU_DTYPE)
    ckv_f = hf[640:768]
    ckv_fn = (ckv_f * _rms_scale(ckv_f, 0) * gkv_col_ref[...]).astype(MXU_DTYPE)
    vtm_ref[0] = _dot(wv_ref[...], ckv_fn).astype(MXU_DTYPE)

    zeros32 = jnp.zeros((DIFF_QK, TM), MXU_DTYPE)
    for c in range(2):
        for rb in range(4):
            r0 = 768 + 128 * c + 32 * rb
            piece = (_rope_rows(hf[r0:r0 + 32], fc32, fs32, 8) * (DIFF_SCALE * LOG2E)).astype(MXU_DTYPE)
            for cb in range(4):
                qd_ref[0, 0, c, 32 * rb:32 * rb + 32, cb * TM:(cb + 1) * TM] = piece if cb == rb else zeros32
    vtd_ref[0] = hf[1024:1280].astype(MXU_DTYPE)

    s_heads = []
    for h in range(4):
        q = hf[1280 + 64 * h:1344 + 64 * h]
        s_heads.append((_rope_rows(q, fc64, fs64, 16) * (GQA_SCALE * LOG2E)).astype(MXU_DTYPE))
    place_heads(qs_ref, s_heads)
    vts_ref[0] = hf[1536:1664].astype(MXU_DTYPE)


def _inproj(xs, modsel, g_pre, w, rope_t, rope_f, *, n_ctx_tiles):
    bsz, t_all, d = xs.shape
    nt = t_all // TM
    bt = lambda t, b: (b, t, 0)
    x_spec = pl.BlockSpec((1, TM, d), bt)
    mod_spec = pl.BlockSpec((1, 1, 9, d), lambda t, b: (b, jnp.where(t < n_ctx_tiles, 0, 1), 0, 0))
    tok = lambda n: (pl.BlockSpec((1, TM, n), bt), jax.ShapeDtypeStruct((bsz, t_all, n), MXU_DTYPE))
    feat = lambda n: (pl.BlockSpec((1, n, TM), lambda t, b: (b, 0, t)),
                      jax.ShapeDtypeStruct((bsz, n, t_all), MXU_DTYPE))
    qblk = (pl.BlockSpec((1, 1, 128, 4 * TM), lambda t, b: (b, t, 0, 0)),
            jax.ShapeDtypeStruct((bsz, nt, 128, 4 * TM), MXU_DTYPE))
    qdblk = (pl.BlockSpec((1, 1, 2, 128, 4 * TM), lambda t, b: (b, t, 0, 0, 0)),
             jax.ShapeDtypeStruct((bsz, nt, 2, 128, 4 * TM), MXU_DTYPE))
    outs = [tok(128), tok(512), tok(256), tok(128), tok(GATE_COLS),
            qblk, feat(128), feat(512), feat(256), qdblk, feat(256), qblk, feat(128)]
    consts = [g_pre, w["w_tok"], w["w_feat_t"], w["w_uq_t"], w["w_k_exp"], w["w_v_t"],
              w["gk_row"], w["gkv_row"], w["gq_col"], w["gmq_col"], w["gkv_col"]]
    return pl.pallas_call(
        _inproj_kernel,
        grid=(nt, bsz),
        in_specs=[x_spec, mod_spec] + [_const_spec(c.shape) for c in consts]
        + [pl.BlockSpec((4, TM, 128), lambda t, b: (0, t, 0)),
           pl.BlockSpec((448, TM), lambda t, b: (0, t))],
        out_specs=[o[0] for o in outs],
        out_shape=[o[1] for o in outs],
        compiler_params=_cparams(2),
        name="mixer_inproj",
    )(xs, modsel, *consts, rope_t, rope_f)


def _value_lhs(vt, r0):
    ones = jnp.ones((ONES_ROWS, vt.shape[1]), MXU_DTYPE)
    return jnp.concatenate([vt[r0:r0 + HEAD_DIM, :], ones], axis=0)


def _softmax_pv(s, col_max, vt, m_ref, acc_ref, u, blocks):
    m_old = m_ref[u]
    m_new = jnp.maximum(m_old, col_max)
    alpha = jnp.exp2(m_old - m_new)
    m_ref[u] = m_new
    p = jnp.exp2(s - m_new).astype(MXU_DTYPE)
    for c0, r0 in blocks:
        pv = _dot(_value_lhs(vt, r0), p[:, c0:c0 + TM])
        acc_ref[u, :, c0:c0 + TM] = alpha[:, c0:c0 + TM] * acc_ref[u, :, c0:c0 + TM] + pv


def _dense_attn_kernel(*refs, units, tk, n_ctx, n_lat, diff):
    if diff:
        q_ref, k_ref, vt_ref, lam_ref, subln_ref, o_ref, sa_ref, sb_ref, cma_ref, cmb_ref, m_ref, acc_ref = refs
    else:
        q_ref, k_ref, vt_ref, o_ref, sa_ref, sb_ref, cma_ref, cmb_ref, m_ref, acc_ref = refs
    qt = pl.program_id(1)
    n_tiles = n_lat // tk
    m_ref[...] = jnp.full(m_ref.shape, NEG_INF, F32)
    acc_ref[...] = jnp.zeros(acc_ref.shape, F32)
    buf_a, buf_b = (sa_ref, cma_ref), (sb_ref, cmb_ref)

    def scores(buf, off, size):
        s_ref, cm_ref = buf
        for u, (q_idx, k0, _) in enumerate(units):
            s = _dot(k_ref[0, pl.ds(off, size), k0:k0 + 128], q_ref[q_idx])
            s_ref[u, 0:size, :] = s
            cm_ref[u] = jnp.max(s, axis=0, keepdims=True)

    def consume(buf, off, size):
        s_ref, cm_ref = buf
        vt = vt_ref[0, :, pl.ds(off, size)]
        for u, (_, _, blocks) in enumerate(units):
            _softmax_pv(s_ref[u, 0:size, :], cm_ref[u], vt, m_ref, acc_ref, u, blocks)

    def lat(j):
        return n_ctx + j * tk if isinstance(j, int) else pl.multiple_of(n_ctx + j * tk, 128)

    scores(buf_a, 0, n_ctx)

    @pl.when(qt == 0)
    def _():
        consume(buf_a, 0, n_ctx)

    @pl.when(qt != 0)
    def _():
        buf_of = lambda parity: buf_a if parity % 2 else buf_b

        def stage(parity, j):
            scores(buf_of(parity + 1), lat(j + 1), tk)
            consume(buf_of(parity), lat(j), tk)

        scores(buf_b, n_ctx, tk)
        consume(buf_a, 0, n_ctx)
        n_loop = (n_tiles - 1) // STAGES_PER_TRIP

        def body(i, carry):
            for r in range(STAGES_PER_TRIP):
                stage(r, STAGES_PER_TRIP * i + r)
            return carry

        lax.fori_loop(0, n_loop, body, 0)
        for j in range(STAGES_PER_TRIP * n_loop, n_tiles - 1):
            stage(j, j)
        consume(buf_of(n_tiles - 1), lat(n_tiles - 1), tk)

    if not diff:
        for u, (_, _, blocks) in enumerate(units):
            inv = 1.0 / acc_ref[u, HEAD_DIM:HEAD_DIM + 1, :]
            for c0, r0 in blocks:
                h = (u * len(blocks) * TM + c0) // TM
                o_ref[0, HEAD_DIM * h:HEAD_DIM * h + HEAD_DIM, :] = acc_ref[u, 0:HEAD_DIM, c0:c0 + TM] * inv[:, c0:c0 + TM]
    else:
        lf = lam_ref[...]
        lam_init = lam_ref[4:5, 0:1]
        lam = (jnp.exp(jnp.sum(lf[0:1, 0:DIFF_QK] * lf[1:2, 0:DIFF_QK], axis=1, keepdims=True))
               - jnp.exp(jnp.sum(lf[2:3, 0:DIFF_QK] * lf[3:4, 0:DIFF_QK], axis=1, keepdims=True)) + lam_init)
        for u in range(len(units)):
            inv = 1.0 / acc_ref[u, HEAD_DIM:HEAD_DIM + 1, :]
            for hh in range(2):
                c1, c2 = 2 * hh * TM, (2 * hh + 1) * TM
                o = (acc_ref[u, 0:HEAD_DIM, c1:c1 + TM] * inv[:, c1:c1 + TM]
                     - lam * (acc_ref[u, 0:HEAD_DIM, c2:c2 + TM] * inv[:, c2:c2 + TM]))
                o = o * _rms_scale(o, 0) * subln_ref[...] * (1.0 - lam_init)
                h = 2 * u + hh
                o_ref[0, HEAD_DIM * h:HEAD_DIM * h + HEAD_DIM, :] = o


def _dense_attention(q, k, vt, *, kind, n_ctx, lam_tab=None, subln_col=None):
    bsz, t_all, dk = k.shape
    nt = t_all // TM
    dv = vt.shape[1]
    n_lat = t_all - n_ctx
    if kind == "gqa":
        q_spec = pl.BlockSpec((1, 1, 128, 4 * TM), lambda b, t: (b, t, 0, 0))
        units = (((0, 0), 0, tuple((h * TM, HEAD_DIM * (h // 2)) for h in range(4))),)
        ncols = 4 * TM
    elif kind == "diff":
        q_spec = pl.BlockSpec((1, 1, 2, 128, 4 * TM), lambda b, t: (b, t, 0, 0, 0))
        units = tuple(((0, 0, u), 128 * u,
                       tuple(((2 * hh + mm) * TM, HEAD_DIM * (2 * u + hh)) for hh in range(2) for mm in range(2)))
                      for u in range(2))
        ncols = 4 * TM
    else:
        q_spec = pl.BlockSpec((1, 512, TM), lambda b, t: (b, 0, t))
        units = tuple(((0, slice(128 * h, 128 * h + 128)), 128 * h, ((0, HEAD_DIM * h),)) for h in range(4))
        ncols = TM
    in_specs = [q_spec,
                pl.BlockSpec((1, t_all, dk), lambda b, t: (b, 0, 0)),
                pl.BlockSpec((1, dv, t_all), lambda b, t: (b, 0, 0))]
    args = [q, k, vt]
    if kind == "diff":
        in_specs += [_const_spec(lam_tab.shape), _const_spec(subln_col.shape)]
        args += [lam_tab, subln_col]
    nu = len(units)
    return pl.pallas_call(
        functools.partial(_dense_attn_kernel, units=units, tk=TK, n_ctx=n_ctx, n_lat=n_lat,
                          diff=(kind == "diff")),
        grid=(bsz, nt),
        in_specs=in_specs,
        out_specs=pl.BlockSpec((1, BRANCH_W, TM), lambda b, t: (b, 0, t)),
        out_shape=jax.ShapeDtypeStruct((bsz, BRANCH_W, t_all), F32),
        scratch_shapes=[pltpu.VMEM((nu, TK, ncols), F32), pltpu.VMEM((nu, TK, ncols), F32),
                        pltpu.VMEM((nu, 1, ncols), F32), pltpu.VMEM((nu, 1, ncols), F32),
                        pltpu.VMEM((nu, 1, ncols), F32), pltpu.VMEM((nu, ACC_ROWS, ncols), F32)],
        compiler_params=_cparams(2),
        name=f"attn_{kind}",
    )(*args)


def _window_attn_kernel(q_ref, k_ref, vt_ref, sink_ref, o_ref, m_ref, acc_ref, *, n_ctx, t_all, wk):
    qt = pl.program_id(1)
    blocks = tuple((h * TM, HEAD_DIM * (h // 2)) for h in range(4))
    m_ref[0] = sink_ref[...] * LOG2E
    acc_ref[0, 0:HEAD_DIM, :] = jnp.zeros((HEAD_DIM, 4 * TM), F32)
    acc_ref[0, HEAD_DIM:ACC_ROWS, :] = jnp.ones((ONES_ROWS, 4 * TM), F32)
    q = q_ref[0, 0]

    s = _dot(k_ref[0, 0:n_ctx, :], q)
    _softmax_pv(s, jnp.max(s, axis=0, keepdims=True), vt_ref[0, :, 0:n_ctx], m_ref, acc_ref, 0, blocks)

    q0 = qt * TM
    start = jnp.clip(q0 - WINDOW, n_ctx, t_all - wk)
    start = pl.multiple_of(start, 128)
    s = _dot(k_ref[0, pl.ds(start, wk), :], q)
    kpos = start + lax.broadcasted_iota(jnp.int32, s.shape, 0)
    qpos = q0 + (lax.broadcasted_iota(jnp.int32, s.shape, 1) & (TM - 1))
    allowed = (jnp.abs(kpos - qpos) <= WINDOW) & (qpos >= n_ctx)
    s = jnp.where(allowed, s, NEG_INF)
    _softmax_pv(s, jnp.max(s, axis=0, keepdims=True), vt_ref[0, :, pl.ds(start, wk)], m_ref, acc_ref, 0, blocks)

    inv = 1.0 / acc_ref[0, HEAD_DIM:HEAD_DIM + 1, :]
    for h, (c0, _) in enumerate(blocks):
        o_ref[0, HEAD_DIM * h:HEAD_DIM * h + HEAD_DIM, :] = acc_ref[0, 0:HEAD_DIM, c0:c0 + TM] * inv[:, c0:c0 + TM]


def _window_attention(q, k, vt, sink_row, *, n_ctx):
    bsz, t_all, dk = k.shape
    nt = t_all // TM
    wk = TM + 2 * WINDOW
    return pl.pallas_call(
        functools.partial(_window_attn_kernel, n_ctx=n_ctx, t_all=t_all, wk=wk),
        grid=(bsz, nt),
        in_specs=[pl.BlockSpec((1, 1, 128, 4 * TM), lambda b, t: (b, t, 0, 0)),
                  pl.BlockSpec((1, t_all, dk), lambda b, t: (b, 0, 0)),
                  pl.BlockSpec((1, vt.shape[1], t_all), lambda b, t: (b, 0, 0)),
                  _const_spec(sink_row.shape)],
        out_specs=pl.BlockSpec((1, BRANCH_W, TM), lambda b, t: (b, 0, t)),
        out_shape=jax.ShapeDtypeStruct((bsz, BRANCH_W, t_all), F32),
        scratch_shapes=[pltpu.VMEM((1, 1, 4 * TM), F32), pltpu.VMEM((1, ACC_ROWS, 4 * TM), F32)],
        compiler_params=_cparams(2),
        name="attn_window",
    )(q, k, vt, sink_row)


def _merge_kernel(x_ref, mod_ref, gpost_ref, oa_ref, om_ref, od_ref, os_ref, g_ref, wb_ref, wo_ref, o_ref):
    x = x_ref[0]
    gate = mod_ref[0, 0, 5:6, :]
    y = None
    for i, o_t in enumerate((oa_ref, om_ref, od_ref, os_ref)):
        o_tok = o_t[0].T.astype(MXU_DTYPE)
        term = g_ref[0, :, i * D_MODEL:(i + 1) * D_MODEL].astype(F32) * _dot(o_tok, wb_ref[i])
        y = term if y is None else y + term
    z = _dot(y.astype(MXU_DTYPE), wo_ref[...])
    o_ref[0] = x + gate * (z * _rms_scale(z, -1) * gpost_ref[...])


def _merge(xs, modsel, g_post, outs, gates, w_branch, w_out, *, n_ctx_tiles):
    bsz, t_all, d = xs.shape
    x_spec, mod_spec = _token_specs(n_ctx_tiles)
    o_spec = pl.BlockSpec((1, BRANCH_W, TM), lambda b, t: (b, 0, t))
    return pl.pallas_call(
        _merge_kernel,
        grid=(bsz, t_all // TM),
        in_specs=[x_spec, mod_spec, _const_spec((1, d)), o_spec, o_spec, o_spec, o_spec,
                  pl.BlockSpec((1, TM, GATE_COLS), lambda b, t: (b, t, 0)),
                  _const_spec(w_branch.shape), _const_spec(w_out.shape)],
        out_specs=x_spec,
        out_shape=jax.ShapeDtypeStruct(xs.shape, F32),
        compiler_params=_cparams(2),
        name="branch_merge",
    )(xs, modsel, g_post, *outs, gates, w_branch, w_out)


def _rope_tables(n_ctx, n_lat):
    pos = jnp.arange(n_lat)
    row = (pos // GRID_W).astype(F32)
    col = (pos % GRID_W).astype(F32)

    def lane_pattern(rot_dim):
        half = rot_dim // 2
        inv_freq = ROPE_THETA ** (-jnp.arange(0, half, 2, dtype=F32) / half)
        ar, ac = row[:, None] * inv_freq[None, :], col[:, None] * inv_freq[None, :]
        cos = jnp.concatenate([jnp.cos(ar), jnp.cos(ar), jnp.cos(ac), jnp.cos(ac)], axis=1)
        sin = jnp.concatenate([-jnp.sin(ar), jnp.sin(ar), -jnp.sin(ac), jnp.sin(ac)], axis=1)
        cos = jnp.concatenate([jnp.ones((n_ctx, rot_dim), F32), cos], axis=0)
        sin = jnp.concatenate([jnp.zeros((n_ctx, rot_dim), F32), sin], axis=0)
        return cos, sin

    c64, s64 = lane_pattern(HEAD_DIM)
    c32, s32 = lane_pattern(DIFF_QK)
    t_all = n_ctx + n_lat
    rope_t = jnp.stack([jnp.tile(c64, (1, 2)), jnp.tile(s64, (1, 2)), jnp.tile(c32, (1, 4)), jnp.tile(s32, (1, 4))])
    ones, zeros = jnp.ones((MLA_NOPE, t_all), F32), jnp.zeros((MLA_NOPE, t_all), F32)
    pad1, pad0 = jnp.ones((32, t_all), F32), jnp.zeros((32, t_all), F32)
    rope_f = jnp.concatenate([c64.T, s64.T, c32.T, s32.T,
                              ones, c32.T, pad1, zeros, s32.T, pad0], axis=0)
    return rope_t, rope_f


def _layer_weights(l, w_in, gqa_q_norm, gqa_k_norm, mla_q_norm, mla_kv_norm, mla_w_uq, mla_w_ukv):
    w = w_in[l]
    col = lambda a, n: w[:, a:a + n]
    kpe = jnp.zeros((D_MODEL, 128), F32).at[:, 64:96].set(col(M0 + 384, 32))
    w_tok = jnp.concatenate([col(A0 + 256, 128), col(M0 + 256, 128), kpe, col(X0 + 256, 256),
                             col(S0 + 256, 128), col(G0, GATE_COLS)], axis=1)
    w_feat = jnp.concatenate([col(A0, 256), col(A0 + 384, 128), col(M0, 256), col(M0 + 256, 128),
                              col(X0, 256), col(X0 + 512, 256), col(S0, 256), col(S0 + 384, 128)], axis=1)
    uq = mla_w_uq[l].reshape(MLA_Q_RANK, MLA_HEADS, MLA_NOPE + MLA_ROPE)
    uq = jnp.pad(uq, ((0, 0), (0, 0), (0, 128 - MLA_NOPE - MLA_ROPE))).reshape(MLA_Q_RANK, 512)
    ukv = mla_w_ukv[l].reshape(MLA_KV_RANK, MLA_HEADS, 2 * MLA_NOPE)
    k_exp = jnp.pad(ukv[:, :, :MLA_NOPE], ((0, 0), (0, 0), (0, 64))).reshape(MLA_KV_RANK, 512)
    v_t = ukv[:, :, MLA_NOPE:].reshape(MLA_KV_RANK, 256).T
    return {
        "w_tok": w_tok.astype(MXU_DTYPE), "w_feat_t": w_feat.T.astype(MXU_DTYPE),
        "w_uq_t": uq.T.astype(MXU_DTYPE), "w_k_exp": k_exp.astype(MXU_DTYPE), "w_v_t": v_t.astype(MXU_DTYPE),
        "gk_row": jnp.tile(gqa_k_norm[l], 2)[None, :], "gkv_row": mla_kv_norm[l][None, :],
        "gq_col": gqa_q_norm[l][:, None], "gmq_col": mla_q_norm[l][:, None], "gkv_col": mla_kv_norm[l][:, None],
    }


def kernel(x, c, ctx, c_ctx, w_mod, b_mod, g_pre, g_post, w_ffn_gate, w_ffn_up, w_ffn_down, w_in,
           gqa_q_norm, gqa_k_norm, mla_q_norm, mla_kv_norm, mla_w_uq, mla_w_ukv,
           diff_lambda, diff_subln, swa_sink, w_branch, w_out):
    bsz, n_lat, d = x.shape
    n_ctx = ctx.shape[1]
    assert d == D_MODEL and n_ctx == TM and n_lat % (2 * TK) == 0 and bsz + 1 <= MOD_ROWS
    n_ctx_tiles = n_ctx // TM

    cvec = jnp.zeros((MOD_ROWS, d), F32).at[:bsz].set(c).at[bsz].set(c_ctx)
    mod = _modulation(cvec, w_mod, b_mod).reshape(DEPTH, MOD_ROWS, 9, d)
    modsel = jnp.stack([jnp.broadcast_to(mod[:, bsz:bsz + 1], (DEPTH, bsz, 9, d)), mod[:, :bsz]], axis=2)

    rope_t, rope_f = _rope_tables(n_ctx, n_lat)
    xs = jnp.concatenate([ctx, x], axis=1)
    cast = lambda a: a.astype(MXU_DTYPE)

    for l in range(DEPTH):
        ffn = lambda v, i, sub: _ffn_half(v, modsel[l], g_pre[l, sub][None], g_post[l, sub][None],
                                          cast(w_ffn_gate[l, i]), cast(w_ffn_up[l, i]), cast(w_ffn_down[l, i]),
                                          sub=sub, n_ctx_tiles=n_ctx_tiles)
        xs = ffn(xs, 0, 0)
        w = _layer_weights(l, w_in, gqa_q_norm, gqa_k_norm, mla_q_norm, mla_kv_norm, mla_w_uq, mla_w_ukv)
        (k_a, k_m, k_d, k_s, gates, q_a, vt_a, q_m, vt_m, q_d, vt_d, q_s, vt_s) = _inproj(
            xs, modsel[l], g_pre[l, 1][None], w, rope_t, rope_f, n_ctx_tiles=n_ctx_tiles)
        lam_init = 0.8 - 0.6 * math.exp(-0.3 * l)
        lam_tab = jnp.zeros((8, 128), F32).at[0:4, 0:DIFF_QK].set(diff_lambda[l]).at[4, :].set(lam_init)
        sink_row = jnp.repeat(swa_sink[l], TM)[None, :]
        outs = (_dense_attention(q_a, k_a, vt_a, kind="gqa", n_ctx=n_ctx),
                _dense_attention(q_m, k_m, vt_m, kind="mla", n_ctx=n_ctx),
                _dense_attention(q_d, k_d, vt_d, kind="diff", n_ctx=n_ctx,
                                 lam_tab=lam_tab, subln_col=diff_subln[l][:, None]),
                _window_attention(q_s, k_s, vt_s, sink_row, n_ctx=n_ctx))
        xs = _merge(xs, modsel[l], g_post[l, 1][None], outs, gates, cast(w_branch[l]), cast(w_out[l]),
                    n_ctx_tiles=n_ctx_tiles)
        xs = ffn(xs, 1, 2)
    return xs[:, n_ctx:, :]
```

```python
import functools
import math

import jax
import jax.numpy as jnp
from jax import lax
from jax.experimental import pallas as pl
from jax.experimental.pallas import tpu as pltpu

MXU_DTYPE = jnp.bfloat16
F32 = jnp.float32

D_MODEL = 1024
D_FF = 2816
DEPTH = 2
GRID_W = 64
WINDOW = 128
ROPE_THETA = 10000.0
EPS = 1e-6
NEG_INF = -1e30
HEAD_DIM = 64
MLA_HEADS = 4
MLA_Q_RANK = 256
MLA_KV_RANK = 128
MLA_NOPE = 64
MLA_ROPE = 32
DIFF_QK = 32
N_BRANCH = 4
BRANCH_W = 256
GQA_SCALE = HEAD_DIM ** -0.5
MLA_SCALE = (MLA_NOPE + MLA_ROPE) ** -0.5
DIFF_SCALE = DIFF_QK ** -0.5
LOG2E = math.log2(math.e)
ONES_ROWS = 16
ACC_ROWS = HEAD_DIM + ONES_ROWS

A0, M0, X0, S0, G0 = 0, 512, 928, 1696, 2208
GATE_COLS = N_BRANCH * D_MODEL
TOK_COLS = 768
FEAT_ROWS = 1664

TM = 256
TK = 512
STAGES_PER_TRIP = 4
MOD_ROWS = 16
MOD_TN = 1024
V7X_VMEM_LIMIT = 56 * 1024 * 1024


def _cparams(n_axes):
    return pltpu.CompilerParams(dimension_semantics=("arbitrary",) * n_axes,
                                vmem_limit_bytes=V7X_VMEM_LIMIT)


def _const_spec(shape):
    n = len(shape)
    return pl.BlockSpec(shape, lambda *_: (0,) * n)


def _sigmoid(v):
    return 1.0 / (1.0 + jnp.exp(-v))


def _dot(a, b):
    return jnp.dot(a, b, preferred_element_type=F32)


def _rms_scale(v, axis):
    return lax.rsqrt(jnp.mean(v * v, axis=axis, keepdims=True) + EPS)


def _mod_kernel(c_ref, w_ref, b_ref, o_ref):
    c = c_ref[...]
    s = (c * _sigmoid(c)).astype(MXU_DTYPE)
    o_ref[0] = _dot(s, w_ref[0].astype(MXU_DTYPE)) + b_ref[0]


def _modulation(cvec, w_mod, b_mod):
    depth, d, n = w_mod.shape
    return pl.pallas_call(
        _mod_kernel,
        grid=(depth, n // MOD_TN),
        in_specs=[pl.BlockSpec((MOD_ROWS, d), lambda l, j: (0, 0)),
                  pl.BlockSpec((1, d, MOD_TN), lambda l, j: (l, 0, j)),
                  pl.BlockSpec((1, 1, MOD_TN), lambda l, j: (l, 0, j))],
        out_specs=pl.BlockSpec((1, MOD_ROWS, MOD_TN), lambda l, j: (l, 0, j)),
        out_shape=jax.ShapeDtypeStruct((depth, MOD_ROWS, n), F32),
        compiler_params=_cparams(2),
        name="modulation",
    )(cvec, w_mod, b_mod.reshape(depth, 1, n))


def _adaln(x, g_row, shift, scale):
    return (x * _rms_scale(x, -1) * g_row) * (1.0 + scale) + shift


def _ffn_kernel(*refs, sub, n_ctx_tiles, split_in):
    if split_in:
        ctx_ref, lat_ref, mod_ref, gpre_ref, gpost_ref, wg_ref, wu_ref, wd_ref, o_ref = refs
        x = jnp.where(pl.program_id(1) < n_ctx_tiles, ctx_ref[0], lat_ref[0])
    else:
        x_ref, mod_ref, gpre_ref, gpost_ref, wg_ref, wu_ref, wd_ref, o_ref = refs
        x = x_ref[0]
    shift = mod_ref[0, 0, 3 * sub:3 * sub + 1, :]
    scale = mod_ref[0, 0, 3 * sub + 1:3 * sub + 2, :]
    gate = mod_ref[0, 0, 3 * sub + 2:3 * sub + 3, :]
    u = _adaln(x, gpre_ref[...], shift, scale).astype(MXU_DTYPE)
    a = _dot(u, wg_ref[...])
    b = _dot(u, wu_ref[...])
    h = ((a * _sigmoid(a)) * b).astype(MXU_DTYPE)
    y = _dot(h, wd_ref[...])
    yn = y * _rms_scale(y, -1) * gpost_ref[...]
    o_ref[0] = x + 0.5 * gate * yn


def _token_specs(n_ctx_tiles):
    x_spec = pl.BlockSpec((1, TM, D_MODEL), lambda b, t: (b, t, 0))
    mod_spec = pl.BlockSpec((1, 1, 9, D_MODEL),
                            lambda b, t: (b, jnp.where(t < n_ctx_tiles, 0, 1), 0, 0))
    return x_spec, mod_spec


def _latent_spec(n_ctx_tiles):
    return pl.BlockSpec((1, TM, D_MODEL), lambda b, t: (b, jnp.maximum(t - n_ctx_tiles, 0), 0))


def _ffn_half(xs, modsel, g_pre, g_post, wg, wu, wd, *, sub, n_ctx_tiles, latent_out=False):
    split_in = isinstance(xs, tuple)
    x_spec, mod_spec = _token_specs(n_ctx_tiles)
    if split_in:
        ctx, lat = xs
        bsz, t_all, d = lat.shape[0], ctx.shape[1] + lat.shape[1], lat.shape[2]
        x_specs = [pl.BlockSpec((1, TM, d), lambda b, t: (b, jnp.minimum(t, n_ctx_tiles - 1), 0)),
                   _latent_spec(n_ctx_tiles)]
        x_args = [ctx, lat]
    else:
        bsz, t_all, d = xs.shape
        x_specs, x_args = [x_spec], [xs]
    n_lat = t_all - n_ctx_tiles * TM
    return pl.pallas_call(
        functools.partial(_ffn_kernel, sub=sub, n_ctx_tiles=n_ctx_tiles, split_in=split_in),
        grid=(bsz, t_all // TM),
        in_specs=x_specs + [mod_spec, _const_spec((1, d)), _const_spec((1, d)),
                            _const_spec(wg.shape), _const_spec(wu.shape), _const_spec(wd.shape)],
        out_specs=_latent_spec(n_ctx_tiles) if latent_out else x_spec,
        out_shape=jax.ShapeDtypeStruct((bsz, n_lat if latent_out else t_all, d), F32),
        compiler_params=_cparams(2),
        name=f"ffn_half_{sub}",
    )(*x_args, modsel, g_pre, g_post, wg, wu, wd)


def _rope_lanes(v, cos, sin, pair):
    width = v.shape[1]
    lane = lax.broadcasted_iota(jnp.int32, v.shape, 1)
    first = (lane & pair) == 0
    partner = jnp.where(first, pltpu.roll(v, width - pair, 1), pltpu.roll(v, pair, 1))
    return v * cos + partner * sin


def _swap_row_blocks(v, pair):
    parts = []
    for r in range(0, v.shape[0], 2 * pair):
        parts.append(v[r + pair:r + 2 * pair])
        parts.append(v[r:r + pair])
    return jnp.concatenate(parts, axis=0)


def _rope_rows(v, cos, sin, pair):
    return v * cos + _swap_row_blocks(v, pair) * sin


def _inproj_kernel(x_ref, mod_ref, gpre_ref, wtok_ref, wft_ref, wuq_ref, wkx_ref, wv_ref,
                   gk_ref, gkv_row_ref, gq_col_ref, gmq_col_ref, gkv_col_ref,
                   rope_t_ref, rope_f_ref,
                   ka_ref, km_ref, kd_ref, ks_ref, g_ref,
                   qa_ref, vta_ref, qm_ref, vtm_ref, qd_ref, vtd_ref, qs_ref, vts_ref):
    x = x_ref[0]
    shift = mod_ref[0, 0, 3:4, :]
    scale = mod_ref[0, 0, 4:5, :]
    u = _adaln(x, gpre_ref[...], shift, scale).astype(MXU_DTYPE)

    c64, s64, c32, s32 = rope_t_ref[0], rope_t_ref[1], rope_t_ref[2], rope_t_ref[3]

    ht = _dot(u, wtok_ref[:, 0:TOK_COLS])
    k_a = ht[:, 0:128]
    sq = k_a * k_a
    lane = lax.broadcasted_iota(jnp.int32, sq.shape, 1)
    low = lane < HEAD_DIM
    ss_lo = jnp.sum(jnp.where(low, sq, 0.0), axis=1, keepdims=True)
    ss_hi = jnp.sum(jnp.where(low, 0.0, sq), axis=1, keepdims=True)
    ms = jnp.where(low, ss_lo, ss_hi) * (1.0 / HEAD_DIM)
    k_a = k_a * lax.rsqrt(ms + EPS) * gk_ref[...]
    ka_ref[0] = _rope_lanes(k_a, c64, s64, 16).astype(MXU_DTYPE)

    ckv = ht[:, 128:256]
    ckv_n = (ckv * _rms_scale(ckv, -1) * gkv_row_ref[...]).astype(MXU_DTYPE)
    k_nope = _dot(ckv_n, wkx_ref[...])
    k_pe = _rope_lanes(ht[:, 256:384], c32, s32, 8)
    for h in range(MLA_HEADS):
        km_ref[0, :, 128 * h:128 * h + 128] = (k_nope[:, 128 * h:128 * h + 128] + k_pe).astype(MXU_DTYPE)

    for j in range(2):
        kd_ref[0, :, 128 * j:128 * j + 128] = _rope_lanes(
            ht[:, 384 + 128 * j:512 + 128 * j], c32, s32, 8).astype(MXU_DTYPE)
    ks_ref[0] = _rope_lanes(ht[:, 640:768], c64, s64, 16).astype(MXU_DTYPE)

    for j in range(GATE_COLS // D_MODEL):
        lo = TOK_COLS + j * D_MODEL
        hg = _dot(u, wtok_ref[:, lo:lo + D_MODEL])
        g_ref[0, :, j * D_MODEL:(j + 1) * D_MODEL] = _sigmoid(hg).astype(g_ref.dtype)

    hf = lax.dot_general(wft_ref[...], u, (((1,), (1,)), ((), ())), preferred_element_type=F32)
    fc64, fs64 = rope_f_ref[0:64, :], rope_f_ref[64:128, :]
    fc32, fs32 = rope_f_ref[128:160, :], rope_f_ref[160:192, :]
    fcm, fsm = rope_f_ref[192:320, :], rope_f_ref[320:448, :]
    zeros64 = jnp.zeros((HEAD_DIM, TM), MXU_DTYPE)

    def place_heads(ref, heads):
        for h, q in enumerate(heads):
            r0 = HEAD_DIM * (h // 2)
            ref[0, 0, r0:r0 + HEAD_DIM, h * TM:(h + 1) * TM] = q
            z0 = HEAD_DIM - r0
            ref[0, 0, z0:z0 + HEAD_DIM, h * TM:(h + 1) * TM] = zeros64

    q_heads = []
    for h in range(4):
        q = hf[64 * h:64 * h + 64]
        q = q * _rms_scale(q, 0) * gq_col_ref[...]
        q_heads.append((_rope_rows(q, fc64, fs64, 16) * (GQA_SCALE * LOG2E)).astype(MXU_DTYPE))
    place_heads(qa_ref, q_heads)
    vta_ref[0] = hf[256:384].astype(MXU_DTYPE)

    cq = hf[384:640]
    cq_n = (cq * _rms_scale(cq, 0) * gmq_col_ref[...]).astype(MXU_DTYPE)
    qm = _dot(wuq_ref[...], cq_n)
    for h in range(MLA_HEADS):
        qh = qm[128 * h:128 * h + 128]
        qm_ref[0, 128 * h:128 * h + 128, :] = (_rope_rows(qh, fcm, fsm, 8) * (MLA_SCALE * LOG2E)).astype(MXU_DTYPE)
    ckv_f = hf[640:768]
    ckv_fn = (ckv_f * _rms_scale(ckv_f, 0) * gkv_col_ref[...]).astype(MXU_DTYPE)
    vtm_ref[0] = _dot(wv_ref[...], ckv_fn).astype(MXU_DTYPE)

    zeros32 = jnp.zeros((DIFF_QK, TM), MXU_DTYPE)
    for c in range(2):
        for rb in range(4):
            r0 = 768 + 128 * c + 32 * rb
            piece = (_rope_rows(hf[r0:r0 + 32], fc32, fs32, 8) * (DIFF_SCALE * LOG2E)).astype(MXU_DTYPE)
            for cb in range(4):
                qd_ref[0, 0, c, 32 * rb:32 * rb + 32, cb * TM:(cb + 1) * TM] = piece if cb == rb else zeros32
    vtd_ref[0] = hf[1024:1280].astype(MXU_DTYPE)

    s_heads = []
    for h in range(4):
        q = hf[1280 + 64 * h:1344 + 64 * h]
        s_heads.append((_rope_rows(q, fc64, fs64, 16) * (GQA_SCALE * LOG2E)).astype(MXU_DTYPE))
    place_heads(qs_ref, s_heads)
    vts_ref[0] = hf[1536:1664].astype(MXU_DTYPE)


def _inproj(xs, modsel, g_pre, w, rope_t, rope_f, *, n_ctx_tiles):
    bsz, t_all, d = xs.shape
    nt = t_all // TM
    bt = lambda t, b: (b, t, 0)
    x_spec = pl.BlockSpec((1, TM, d), bt)
    mod_spec = pl.BlockSpec((1, 1, 9, d), lambda t, b: (b, jnp.where(t < n_ctx_tiles, 0, 1), 0, 0))
    tok = lambda n: (pl.BlockSpec((1, TM, n), bt), jax.ShapeDtypeStruct((bsz, t_all, n), MXU_DTYPE))
    feat = lambda n: (pl.BlockSpec((1, n, TM), lambda t, b: (b, 0, t)),
                      jax.ShapeDtypeStruct((bsz, n, t_all), MXU_DTYPE))
    qblk = (pl.BlockSpec((1, 1, 128, 4 * TM), lambda t, b: (b, t, 0, 0)),
            jax.ShapeDtypeStruct((bsz, nt, 128, 4 * TM), MXU_DTYPE))
    qdblk = (pl.BlockSpec((1, 1, 2, 128, 4 * TM), lambda t, b: (b, t, 0, 0, 0)),
             jax.ShapeDtypeStruct((bsz, nt, 2, 128, 4 * TM), MXU_DTYPE))
    outs = [tok(128), tok(512), tok(256), tok(128), tok(GATE_COLS),
            qblk, feat(128), feat(512), feat(256), qdblk, feat(256), qblk, feat(128)]
    consts = [g_pre, w["w_tok"], w["w_feat_t"], w["w_uq_t"], w["w_k_exp"], w["w_v_t"],
              w["gk_row"], w["gkv_row"], w["gq_col"], w["gmq_col"], w["gkv_col"]]
    return pl.pallas_call(
        _inproj_kernel,
        grid=(nt, bsz),
        in_specs=[x_spec, mod_spec] + [_const_spec(c.shape) for c in consts]
        + [pl.BlockSpec((4, TM, 128), lambda t, b: (0, t, 0)),
           pl.BlockSpec((448, TM), lambda t, b: (0, t))],
        out_specs=[o[0] for o in outs],
        out_shape=[o[1] for o in outs],
        compiler_params=_cparams(2),
        name="mixer_inproj",
    )(xs, modsel, *consts, rope_t, rope_f)


def _value_lhs(vt, r0):
    ones = jnp.ones((ONES_ROWS, vt.shape[1]), MXU_DTYPE)
    return jnp.concatenate([vt[r0:r0 + HEAD_DIM, :], ones], axis=0)


def _dense_attn_kernel(*refs, units, tk, n_ctx, n_lat, diff):
    if diff:
        (q1_ref, q2_ref, k_ref, vt_ref, lam_ref, subln_ref, o_ref,
         sa_ref, sb_ref, cma_ref, cmb_ref, m_ref, acc_ref) = refs
    else:
        q1_ref, q2_ref, k_ref, vt_ref, o_ref, sa_ref, sb_ref, cma_ref, cmb_ref, m_ref, acc_ref = refs
    step = pl.program_id(1)
    n_tiles = n_lat // tk
    nu = len(units)
    q_refs = (q1_ref, q2_ref)
    both, second = (0, 1), (1,)
    m_ref[...] = jnp.full(m_ref.shape, NEG_INF, F32)
    acc_ref[...] = jnp.zeros(acc_ref.shape, F32)
    buf_a, buf_b = (sa_ref, cma_ref), (sb_ref, cmb_ref)

    col_blocks = [(qi, u, c0, r0) for qi in both for u, (_, _, blocks) in enumerate(units) for c0, r0 in blocks]

    def scores_block(buf, off, size, qi, u, c0):
        s_ref, cm_ref = buf
        q_idx, k0, _ = units[u]
        s = _dot(k_ref[0, pl.ds(off, size), k0:k0 + 128], q_refs[qi][q_idx][:, c0:c0 + TM])
        s_ref[qi * nu + u, 0:size, c0:c0 + TM] = s
        cm_ref[qi * nu + u, :, c0:c0 + TM] = jnp.max(s, axis=0, keepdims=True)

    def consume_block(buf, off, size, qi, u, c0, r0):
        s_ref, cm_ref = buf
        w = qi * nu + u
        m_old = m_ref[w, :, c0:c0 + TM]
        m_new = jnp.maximum(m_old, cm_ref[w, :, c0:c0 + TM])
        alpha = jnp.exp2(m_old - m_new)
        m_ref[w, :, c0:c0 + TM] = m_new
        p = jnp.exp2(s_ref[w, 0:size, c0:c0 + TM] - m_new).astype(MXU_DTYPE)
        pv = _dot(_value_lhs(vt_ref[0, :, pl.ds(off, size)], r0), p)
        acc_ref[w, :, c0:c0 + TM] = alpha * acc_ref[w, :, c0:c0 + TM] + pv

    def scores(buf, off, size, halves):
        for qi, u, c0, _ in col_blocks:
            if qi in halves:
                scores_block(buf, off, size, qi, u, c0)

    def consume(buf, off, size, halves):
        for qi, u, c0, r0 in col_blocks:
            if qi in halves:
                consume_block(buf, off, size, qi, u, c0, r0)

    def finalize(halves):
        if diff:
            lf = lam_ref[...]
            lam_init = lam_ref[4:5, 0:1]
            lam = (jnp.exp(jnp.sum(lf[0:1, 0:DIFF_QK] * lf[1:2, 0:DIFF_QK], axis=1, keepdims=True))
                   - jnp.exp(jnp.sum(lf[2:3, 0:DIFF_QK] * lf[3:4, 0:DIFF_QK], axis=1, keepdims=True)) + lam_init)
        for qi in halves:
            out0 = qi * TM
            for u, (_, _, blocks) in enumerate(units):
                w = qi * nu + u
                inv = 1.0 / acc_ref[w, HEAD_DIM:HEAD_DIM + 1, :]
                if diff:
                    for hh in range(2):
                        c1, c2 = 2 * hh * TM, (2 * hh + 1) * TM
                        o = (acc_ref[w, 0:HEAD_DIM, c1:c1 + TM] * inv[:, c1:c1 + TM]
                             - lam * (acc_ref[w, 0:HEAD_DIM, c2:c2 + TM] * inv[:, c2:c2 + TM]))
                        o = o * _rms_scale(o, 0) * subln_ref[...] * (1.0 - lam_init)
                        h = 2 * u + hh
                        o_ref[0, HEAD_DIM * h:HEAD_DIM * h + HEAD_DIM, out0:out0 + TM] = o
                else:
                    for c0, _ in blocks:
                        h = (u * len(blocks) * TM + c0) // TM
                        o_ref[0, HEAD_DIM * h:HEAD_DIM * h + HEAD_DIM, out0:out0 + TM] = (
                            acc_ref[w, 0:HEAD_DIM, c0:c0 + TM] * inv[:, c0:c0 + TM])

    def lat(j):
        return n_ctx + j * tk if isinstance(j, int) else pl.multiple_of(n_ctx + j * tk, 128)

    @pl.when(step == 0)
    def _():
        scores(buf_a, 0, n_ctx, second)
        consume(buf_a, 0, n_ctx, second)
        o_ref[0, :, 0:TM] = jnp.zeros((BRANCH_W, TM), F32)
        finalize(second)

    @pl.when(step != 0)
    def _():
        buf_of = lambda parity: buf_a if parity % 2 else buf_b

        def stage(parity, j):
            for qi, u, c0, r0 in col_blocks:
                scores_block(buf_of(parity + 1), lat(j + 1), tk, qi, u, c0)
                consume_block(buf_of(parity), lat(j), tk, qi, u, c0, r0)

        scores(buf_a, 0, n_ctx, both)
        for qi, u, c0, r0 in col_blocks:
            scores_block(buf_b, n_ctx, tk, qi, u, c0)
            consume_block(buf_a, 0, n_ctx, qi, u, c0, r0)
        n_loop = (n_tiles - 1) // STAGES_PER_TRIP

        def body(i, carry):
            for r in range(STAGES_PER_TRIP):
                stage(r, STAGES_PER_TRIP * i + r)
            return carry

        lax.fori_loop(0, n_loop, body, 0)
        for j in range(STAGES_PER_TRIP * n_loop, n_tiles - 1):
            stage(j, j)
        consume(buf_of(n_tiles - 1), lat(n_tiles - 1), tk, both)
        finalize(both)


def _dense_attention(q, k, vt, *, kind, n_ctx, lam_tab=None, subln_col=None):
    bsz, t_all, dk = k.shape
    dv = vt.shape[1]
    n_lat = t_all - n_ctx
    n_steps = 1 + (n_lat // TM) // 2
    tiles = (lambda t: jnp.maximum(2 * t - 1, 0), lambda t: 2 * t)
    if kind == "gqa":
        q_specs = [pl.BlockSpec((1, 1, 128, 4 * TM), lambda b, t, f=f: (b, f(t), 0, 0)) for f in tiles]
        units = (((0, 0), 0, tuple((h * TM, HEAD_DIM * (h // 2)) for h in range(4))),)
        ncols = 4 * TM
    elif kind == "diff":
        q_specs = [pl.BlockSpec((1, 1, 2, 128, 4 * TM), lambda b, t, f=f: (b, f(t), 0, 0, 0)) for f in tiles]
        units = tuple(((0, 0, u), 128 * u,
                       tuple(((2 * hh + mm) * TM, HEAD_DIM * (2 * u + hh)) for hh in range(2) for mm in range(2)))
                      for u in range(2))
        ncols = 4 * TM
    else:
        q_specs = [pl.BlockSpec((1, 512, TM), lambda b, t, f=f: (b, 0, f(t))) for f in tiles]
        units = tuple(((0, slice(128 * h, 128 * h + 128)), 128 * h, ((0, HEAD_DIM * h),)) for h in range(4))
        ncols = TM
    in_specs = q_specs + [pl.BlockSpec((1, t_all, dk), lambda b, t: (b, 0, 0)),
                          pl.BlockSpec((1, dv, t_all), lambda b, t: (b, 0, 0))]
    args = [q, q, k, vt]
    if kind == "diff":
        in_specs += [_const_spec(lam_tab.shape), _const_spec(subln_col.shape)]
        args += [lam_tab, subln_col]
    nw = 2 * len(units)
    return pl.pallas_call(
        functools.partial(_dense_attn_kernel, units=units, tk=TK, n_ctx=n_ctx, n_lat=n_lat,
                          diff=(kind == "diff")),
        grid=(bsz, n_steps),
        in_specs=in_specs,
        out_specs=pl.BlockSpec((1, BRANCH_W, 2 * TM), lambda b, t: (b, 0, t)),
        out_shape=jax.ShapeDtypeStruct((bsz, BRANCH_W, 2 * TM * n_steps), F32),
        scratch_shapes=[pltpu.VMEM((nw, TK, ncols), F32), pltpu.VMEM((nw, TK, ncols), F32),
                        pltpu.VMEM((nw, 1, ncols), F32), pltpu.VMEM((nw, 1, ncols), F32),
                        pltpu.VMEM((nw, 1, ncols), F32), pltpu.VMEM((nw, ACC_ROWS, ncols), F32)],
        compiler_params=_cparams(2),
        name=f"attn_{kind}",
    )(*args)


def _window_attn_kernel(q_ref, k_ref, vt_ref, sink_ref, o_ref, *, n_ctx, t_all, wk):
    qt = pl.program_id(1)
    q0 = qt * TM
    start = pl.multiple_of(jnp.clip(q0 - WINDOW, n_ctx, t_all - wk), 128)
    kpos = start + lax.broadcasted_iota(jnp.int32, (wk, TM), 0)
    qpos = q0 + lax.broadcasted_iota(jnp.int32, (wk, TM), 1)
    allowed = (jnp.abs(kpos - qpos) <= WINDOW) & (qpos >= n_ctx)
    k_ctx, k_win = k_ref[0, 0:n_ctx, :], k_ref[0, pl.ds(start, wk), :]
    vt_ctx, vt_win = vt_ref[0, :, 0:n_ctx], vt_ref[0, :, pl.ds(start, wk)]
    heads = range(4)
    s_all = [(_dot(k_ctx, q_ref[0, 0, :, h * TM:(h + 1) * TM]),
              jnp.where(allowed, _dot(k_win, q_ref[0, 0, :, h * TM:(h + 1) * TM]), NEG_INF)) for h in heads]
    for h in heads:
        c0, r0 = h * TM, HEAD_DIM * (h // 2)
        s_ctx, s_win = s_all[h]
        sink = sink_ref[:, c0:c0 + TM] * LOG2E
        m = jnp.maximum(jnp.maximum(jnp.max(s_ctx, axis=0, keepdims=True),
                                    jnp.max(s_win, axis=0, keepdims=True)), sink)
        acc = (_dot(_value_lhs(vt_ctx, r0), jnp.exp2(s_ctx - m).astype(MXU_DTYPE))
               + _dot(_value_lhs(vt_win, r0), jnp.exp2(s_win - m).astype(MXU_DTYPE)))
        denom = acc[HEAD_DIM:HEAD_DIM + 1, :] + jnp.exp2(sink - m)
        o_ref[0, HEAD_DIM * h:HEAD_DIM * h + HEAD_DIM, :] = acc[0:HEAD_DIM, :] * (1.0 / denom)


def _window_attention(q, k, vt, sink_row, *, n_ctx):
    bsz, t_all, dk = k.shape
    nt = t_all // TM
    wk = TM + 2 * WINDOW
    return pl.pallas_call(
        functools.partial(_window_attn_kernel, n_ctx=n_ctx, t_all=t_all, wk=wk),
        grid=(bsz, nt),
        in_specs=[pl.BlockSpec((1, 1, 128, 4 * TM), lambda b, t: (b, t, 0, 0)),
                  pl.BlockSpec((1, t_all, dk), lambda b, t: (b, 0, 0)),
                  pl.BlockSpec((1, vt.shape[1], t_all), lambda b, t: (b, 0, 0)),
                  _const_spec(sink_row.shape)],
        out_specs=pl.BlockSpec((1, BRANCH_W, TM), lambda b, t: (b, 0, t)),
        out_shape=jax.ShapeDtypeStruct((bsz, BRANCH_W, t_all), F32),
        compiler_params=_cparams(2),
        name="attn_window",
    )(q, k, vt, sink_row)


def _merge_kernel(x_ref, mod_ref, gpost_ref, oa_ref, om_ref, od_ref, os_ref, g_ref, wb_ref, wo_ref, o_ref):
    x = x_ref[0]
    gate = mod_ref[0, 0, 5:6, :]
    y = None
    for i, o_t in enumerate((oa_ref, om_ref, od_ref, os_ref)):
        o_tok = o_t[0].T.astype(MXU_DTYPE)
        term = g_ref[0, :, i * D_MODEL:(i + 1) * D_MODEL].astype(F32) * _dot(o_tok, wb_ref[i])
        y = term if y is None else y + term
    z = _dot(y.astype(MXU_DTYPE), wo_ref[...])
    o_ref[0] = x + gate * (z * _rms_scale(z, -1) * gpost_ref[...])


def _merge(xs, modsel, g_post, outs, gates, w_branch, w_out, *, n_ctx_tiles):
    bsz, t_all, d = xs.shape
    x_spec, mod_spec = _token_specs(n_ctx_tiles)
    o_dense = pl.BlockSpec((1, BRANCH_W, TM), lambda b, t: (b, 0, t + 1))
    o_window = pl.BlockSpec((1, BRANCH_W, TM), lambda b, t: (b, 0, t))
    return pl.pallas_call(
        _merge_kernel,
        grid=(bsz, t_all // TM),
        in_specs=[x_spec, mod_spec, _const_spec((1, d)), o_dense, o_dense, o_dense, o_window,
                  pl.BlockSpec((1, TM, GATE_COLS), lambda b, t: (b, t, 0)),
                  _const_spec(w_branch.shape), _const_spec(w_out.shape)],
        out_specs=x_spec,
        out_shape=jax.ShapeDtypeStruct(xs.shape, F32),
        compiler_params=_cparams(2),
        name="branch_merge",
    )(xs, modsel, g_post, *outs, gates, w_branch, w_out)


def _rope_tables(n_ctx, n_lat):
    pos = jnp.arange(n_lat)
    row = (pos // GRID_W).astype(F32)
    col = (pos % GRID_W).astype(F32)

    def lane_pattern(rot_dim):
        half = rot_dim // 2
        inv_freq = ROPE_THETA ** (-jnp.arange(0, half, 2, dtype=F32) / half)
        ar, ac = row[:, None] * inv_freq[None, :], col[:, None] * inv_freq[None, :]
        cos = jnp.concatenate([jnp.cos(ar), jnp.cos(ar), jnp.cos(ac), jnp.cos(ac)], axis=1)
        sin = jnp.concatenate([-jnp.sin(ar), jnp.sin(ar), -jnp.sin(ac), jnp.sin(ac)], axis=1)
        cos = jnp.concatenate([jnp.ones((n_ctx, rot_dim), F32), cos], axis=0)
        sin = jnp.concatenate([jnp.zeros((n_ctx, rot_dim), F32), sin], axis=0)
        return cos, sin

    c64, s64 = lane_pattern(HEAD_DIM)
    c32, s32 = lane_pattern(DIFF_QK)
    t_all = n_ctx + n_lat
    rope_t = jnp.stack([jnp.tile(c64, (1, 2)), jnp.tile(s64, (1, 2)), jnp.tile(c32, (1, 4)), jnp.tile(s32, (1, 4))])
    ones, zeros = jnp.ones((MLA_NOPE, t_all), F32), jnp.zeros((MLA_NOPE, t_all), F32)
    pad1, pad0 = jnp.ones((32, t_all), F32), jnp.zeros((32, t_all), F32)
    rope_f = jnp.concatenate([c64.T, s64.T, c32.T, s32.T,
                              ones, c32.T, pad1, zeros, s32.T, pad0], axis=0)
    return rope_t, rope_f


def _layer_weights(l, w_in, gqa_q_norm, gqa_k_norm, mla_q_norm, mla_kv_norm, mla_w_uq, mla_w_ukv):
    w = w_in[l]
    col = lambda a, n: w[:, a:a + n]
    kpe = jnp.zeros((D_MODEL, 128), F32).at[:, 64:96].set(col(M0 + 384, 32))
    w_tok = jnp.concatenate([col(A0 + 256, 128), col(M0 + 256, 128), kpe, col(X0 + 256, 256),
                             col(S0 + 256, 128), col(G0, GATE_COLS)], axis=1)
    w_feat = jnp.concatenate([col(A0, 256), col(A0 + 384, 128), col(M0, 256), col(M0 + 256, 128),
                              col(X0, 256), col(X0 + 512, 256), col(S0, 256), col(S0 + 384, 128)], axis=1)
    uq = mla_w_uq[l].reshape(MLA_Q_RANK, MLA_HEADS, MLA_NOPE + MLA_ROPE)
    uq = jnp.pad(uq, ((0, 0), (0, 0), (0, 128 - MLA_NOPE - MLA_ROPE))).reshape(MLA_Q_RANK, 512)
    ukv = mla_w_ukv[l].reshape(MLA_KV_RANK, MLA_HEADS, 2 * MLA_NOPE)
    k_exp = jnp.pad(ukv[:, :, :MLA_NOPE], ((0, 0), (0, 0), (0, 64))).reshape(MLA_KV_RANK, 512)
    v_t = ukv[:, :, MLA_NOPE:].reshape(MLA_KV_RANK, 256).T
    return {
        "w_tok": w_tok.astype(MXU_DTYPE), "w_feat_t": w_feat.T.astype(MXU_DTYPE),
        "w_uq_t": uq.T.astype(MXU_DTYPE), "w_k_exp": k_exp.astype(MXU_DTYPE), "w_v_t": v_t.astype(MXU_DTYPE),
        "gk_row": jnp.tile(gqa_k_norm[l], 2)[None, :], "gkv_row": mla_kv_norm[l][None, :],
        "gq_col": gqa_q_norm[l][:, None], "gmq_col": mla_q_norm[l][:, None], "gkv_col": mla_kv_norm[l][:, None],
    }


def kernel(x, c, ctx, c_ctx, w_mod, b_mod, g_pre, g_post, w_ffn_gate, w_ffn_up, w_ffn_down, w_in,
           gqa_q_norm, gqa_k_norm, mla_q_norm, mla_kv_norm, mla_w_uq, mla_w_ukv,
           diff_lambda, diff_subln, swa_sink, w_branch, w_out):
    bsz, n_lat, d = x.shape
    n_ctx = ctx.shape[1]
    assert d == D_MODEL and n_ctx == TM and n_lat % (2 * TK) == 0 and bsz + 1 <= MOD_ROWS
    n_ctx_tiles = n_ctx // TM

    cvec = jnp.zeros((MOD_ROWS, d), F32).at[:bsz].set(c).at[bsz].set(c_ctx)
    mod = _modulation(cvec, w_mod, b_mod).reshape(DEPTH, MOD_ROWS, 9, d)
    modsel = jnp.stack([jnp.broadcast_to(mod[:, bsz:bsz + 1], (DEPTH, bsz, 9, d)), mod[:, :bsz]], axis=2)

    rope_t, rope_f = _rope_tables(n_ctx, n_lat)
    xs = (ctx, x)
    cast = lambda a: a.astype(MXU_DTYPE)

    for l in range(DEPTH):
        ffn = lambda v, i, sub, **kw: _ffn_half(v, modsel[l], g_pre[l, sub][None], g_post[l, sub][None],
                                                cast(w_ffn_gate[l, i]), cast(w_ffn_up[l, i]), cast(w_ffn_down[l, i]),
                                                sub=sub, n_ctx_tiles=n_ctx_tiles, **kw)
        xs = ffn(xs, 0, 0)
        w = _layer_weights(l, w_in, gqa_q_norm, gqa_k_norm, mla_q_norm, mla_kv_norm, mla_w_uq, mla_w_ukv)
        (k_a, k_m, k_d, k_s, gates, q_a, vt_a, q_m, vt_m, q_d, vt_d, q_s, vt_s) = _inproj(
            xs, modsel[l], g_pre[l, 1][None], w, rope_t, rope_f, n_ctx_tiles=n_ctx_tiles)
        lam_init = 0.8 - 0.6 * math.exp(-0.3 * l)
        lam_tab = jnp.zeros((8, 128), F32).at[0:4, 0:DIFF_QK].set(diff_lambda[l]).at[4, :].set(lam_init)
        sink_row = jnp.repeat(swa_sink[l], TM)[None, :]
        outs = (_dense_attention(q_a, k_a, vt_a, kind="gqa", n_ctx=n_ctx),
                _dense_attention(q_m, k_m, vt_m, kind="mla", n_ctx=n_ctx),
                _dense_attention(q_d, k_d, vt_d, kind="diff", n_ctx=n_ctx,
                                 lam_tab=lam_tab, subln_col=diff_subln[l][:, None]),
                _window_attention(q_s, k_s, vt_s, sink_row, n_ctx=n_ctx))
        xs = _merge(xs, modsel[l], g_post[l, 1][None], outs, gates, cast(w_branch[l]), cast(w_out[l]),
                    n_ctx_tiles=n_ctx_tiles)
        xs = ffn(xs, 1, 2, latent_out=(l == DEPTH - 1))
    return xs
```

```python
import functools
import math

import jax
import jax.numpy as jnp
from jax import lax
from jax.experimental import pallas as pl
from jax.experimental.pallas import tpu as pltpu

MXU_DTYPE = jnp.bfloat16
F32 = jnp.float32

D_MODEL = 1024
D_FF = 2816
DEPTH = 2
GRID_W = 64
WINDOW = 128
ROPE_THETA = 10000.0
EPS = 1e-6
NEG_INF = -1e30
HEAD_DIM = 64
MLA_HEADS = 4
MLA_Q_RANK = 256
MLA_KV_RANK = 128
MLA_NOPE = 64
MLA_ROPE = 32
DIFF_QK = 32
N_BRANCH = 4
BRANCH_W = 256
GQA_SCALE = HEAD_DIM ** -0.5
MLA_SCALE = (MLA_NOPE + MLA_ROPE) ** -0.5
DIFF_SCALE = DIFF_QK ** -0.5
LOG2E = math.log2(math.e)
ONES_ROWS = 16
ACC_ROWS = HEAD_DIM + ONES_ROWS

A0, M0, X0, S0, G0 = 0, 512, 928, 1696, 2208
GATE_COLS = N_BRANCH * D_MODEL
TOK_COLS = 768
FEAT_ROWS = 1664

TM = 256
TK = 512
STAGES_PER_TRIP = 4
MOD_ROWS = 16
MOD_TN = 1024
V7X_VMEM_LIMIT = 56 * 1024 * 1024


def _cparams(n_axes):
    return pltpu.CompilerParams(dimension_semantics=("arbitrary",) * n_axes,
                                vmem_limit_bytes=V7X_VMEM_LIMIT)


def _const_spec(shape):
    n = len(shape)
    return pl.BlockSpec(shape, lambda *_: (0,) * n, pipeline_mode=pl.Buffered(1))


def _sigmoid(v):
    return 1.0 / (1.0 + jnp.exp(-v))


def _dot(a, b):
    return jnp.dot(a, b, preferred_element_type=F32)


def _rms_scale(v, axis):
    return lax.rsqrt(jnp.mean(v * v, axis=axis, keepdims=True) + EPS)


def _mod_kernel(c_ref, w_ref, b_ref, o_ref):
    c = c_ref[...]
    s = (c * _sigmoid(c)).astype(MXU_DTYPE)
    o_ref[0] = _dot(s, w_ref[0].astype(MXU_DTYPE)) + b_ref[0]


def _modulation(cvec, w_mod, b_mod):
    depth, d, n = w_mod.shape
    return pl.pallas_call(
        _mod_kernel,
        grid=(depth, n // MOD_TN),
        in_specs=[pl.BlockSpec((MOD_ROWS, d), lambda l, j: (0, 0)),
                  pl.BlockSpec((1, d, MOD_TN), lambda l, j: (l, 0, j)),
                  pl.BlockSpec((1, 1, MOD_TN), lambda l, j: (l, 0, j))],
        out_specs=pl.BlockSpec((1, MOD_ROWS, MOD_TN), lambda l, j: (l, 0, j)),
        out_shape=jax.ShapeDtypeStruct((depth, MOD_ROWS, n), F32),
        compiler_params=_cparams(2),
        name="modulation",
    )(cvec, w_mod, b_mod.reshape(depth, 1, n))


def _adaln(x, g_row, shift, scale):
    return (x * _rms_scale(x, -1) * g_row) * (1.0 + scale) + shift


def _ffn_residual(x, mod_ref, gpre_ref, gpost_ref, wg_ref, wu_ref, wd_ref, sub):
    shift = mod_ref[0, 0, 3 * sub:3 * sub + 1, :]
    scale = mod_ref[0, 0, 3 * sub + 1:3 * sub + 2, :]
    gate = mod_ref[0, 0, 3 * sub + 2:3 * sub + 3, :]
    u = _adaln(x, gpre_ref[...], shift, scale).astype(MXU_DTYPE)
    a = _dot(u, wg_ref[...])
    b = _dot(u, wu_ref[...])
    h = ((a * _sigmoid(a)) * b).astype(MXU_DTYPE)
    y = _dot(h, wd_ref[...])
    yn = y * _rms_scale(y, -1) * gpost_ref[...]
    return x + 0.5 * gate * yn


def _ffn_kernel(*refs, sub, n_ctx_tiles, split_in):
    if split_in:
        ctx_ref, lat_ref, mod_ref, gpre_ref, gpost_ref, wg_ref, wu_ref, wd_ref, o_ref = refs
        x = jnp.where(pl.program_id(1) < n_ctx_tiles, ctx_ref[0], lat_ref[0])
    else:
        x_ref, mod_ref, gpre_ref, gpost_ref, wg_ref, wu_ref, wd_ref, o_ref = refs
        x = x_ref[0]
    o_ref[0] = _ffn_residual(x, mod_ref, gpre_ref, gpost_ref, wg_ref, wu_ref, wd_ref, sub)


def _token_specs(n_ctx_tiles):
    x_spec = pl.BlockSpec((1, TM, D_MODEL), lambda b, t: (b, t, 0))
    mod_spec = pl.BlockSpec((1, 1, 9, D_MODEL),
                            lambda b, t: (b, jnp.where(t < n_ctx_tiles, 0, 1), 0, 0))
    return x_spec, mod_spec


def _latent_spec(n_ctx_tiles):
    return pl.BlockSpec((1, TM, D_MODEL), lambda b, t: (b, jnp.maximum(t - n_ctx_tiles, 0), 0))


def _ffn_half(xs, modsel, g_pre, g_post, wg, wu, wd, *, sub, n_ctx_tiles):
    split_in = isinstance(xs, tuple)
    x_spec, mod_spec = _token_specs(n_ctx_tiles)
    if split_in:
        ctx, lat = xs
        bsz, t_all, d = lat.shape[0], ctx.shape[1] + lat.shape[1], lat.shape[2]
        x_specs = [pl.BlockSpec((1, TM, d), lambda b, t: (b, jnp.minimum(t, n_ctx_tiles - 1), 0)),
                   _latent_spec(n_ctx_tiles)]
        x_args = [ctx, lat]
    else:
        bsz, t_all, d = xs.shape
        x_specs, x_args = [x_spec], [xs]
    return pl.pallas_call(
        functools.partial(_ffn_kernel, sub=sub, n_ctx_tiles=n_ctx_tiles, split_in=split_in),
        grid=(bsz, t_all // TM),
        in_specs=x_specs + [mod_spec, _const_spec((1, d)), _const_spec((1, d)),
                            _const_spec(wg.shape), _const_spec(wu.shape), _const_spec(wd.shape)],
        out_specs=x_spec,
        out_shape=jax.ShapeDtypeStruct((bsz, t_all, d), F32),
        compiler_params=_cparams(2),
        name=f"ffn_half_{sub}",
    )(*x_args, modsel, g_pre, g_post, wg, wu, wd)


def _rope_lanes(v, cos, sin, pair):
    width = v.shape[1]
    lane = lax.broadcasted_iota(jnp.int32, v.shape, 1)
    first = (lane & pair) == 0
    partner = jnp.where(first, pltpu.roll(v, width - pair, 1), pltpu.roll(v, pair, 1))
    return v * cos + partner * sin


def _swap_row_blocks(v, pair):
    parts = []
    for r in range(0, v.shape[0], 2 * pair):
        parts.append(v[r + pair:r + 2 * pair])
        parts.append(v[r:r + pair])
    return jnp.concatenate(parts, axis=0)


def _rope_rows(v, cos, sin, pair):
    return v * cos + _swap_row_blocks(v, pair) * sin


def _inproj_kernel(x_ref, mod_ref, gpre_ref, wtok_ref, wft_ref, wuq_ref, wkx_ref, wv_ref,
                   gk_ref, gkv_row_ref, gq_col_ref, gmq_col_ref, gkv_col_ref,
                   rope_t_ref, rope_f_ref,
                   ka_ref, km_ref, kd_ref, ks_ref, g_ref,
                   qa_ref, vta_ref, qm_ref, vtm_ref, qd_ref, vtd_ref, qs_ref, vts_ref):
    x = x_ref[0]
    shift = mod_ref[0, 0, 3:4, :]
    scale = mod_ref[0, 0, 4:5, :]
    u = _adaln(x, gpre_ref[...], shift, scale).astype(MXU_DTYPE)

    c64, s64, c32, s32 = rope_t_ref[0], rope_t_ref[1], rope_t_ref[2], rope_t_ref[3]

    ht = _dot(u, wtok_ref[:, 0:TOK_COLS])
    k_a = ht[:, 0:128]
    sq = k_a * k_a
    lane = lax.broadcasted_iota(jnp.int32, sq.shape, 1)
    low = lane < HEAD_DIM
    ss_lo = jnp.sum(jnp.where(low, sq, 0.0), axis=1, keepdims=True)
    ss_hi = jnp.sum(jnp.where(low, 0.0, sq), axis=1, keepdims=True)
    ms = jnp.where(low, ss_lo, ss_hi) * (1.0 / HEAD_DIM)
    k_a = k_a * lax.rsqrt(ms + EPS) * gk_ref[...]
    ka_ref[0] = _rope_lanes(k_a, c64, s64, 16).astype(MXU_DTYPE)

    ckv = ht[:, 128:256]
    ckv_n = (ckv * _rms_scale(ckv, -1) * gkv_row_ref[...]).astype(MXU_DTYPE)
    k_nope = _dot(ckv_n, wkx_ref[...])
    k_pe = _rope_lanes(ht[:, 256:384], c32, s32, 8)
    for h in range(MLA_HEADS):
        km_ref[0, :, 128 * h:128 * h + 128] = (k_nope[:, 128 * h:128 * h + 128] + k_pe).astype(MXU_DTYPE)

    for j in range(2):
        kd_ref[0, :, 128 * j:128 * j + 128] = _rope_lanes(
            ht[:, 384 + 128 * j:512 + 128 * j], c32, s32, 8).astype(MXU_DTYPE)
    ks_ref[0] = _rope_lanes(ht[:, 640:768], c64, s64, 16).astype(MXU_DTYPE)

    for j in range(GATE_COLS // D_MODEL):
        lo = TOK_COLS + j * D_MODEL
        hg = _dot(u, wtok_ref[:, lo:lo + D_MODEL])
        g_ref[0, :, j * D_MODEL:(j + 1) * D_MODEL] = _sigmoid(hg).astype(g_ref.dtype)

    hf = lax.dot_general(wft_ref[...], u, (((1,), (1,)), ((), ())), preferred_element_type=F32)
    fc64, fs64 = rope_f_ref[0:64, :], rope_f_ref[64:128, :]
    fc32, fs32 = rope_f_ref[128:160, :], rope_f_ref[160:192, :]
    fcm, fsm = rope_f_ref[192:320, :], rope_f_ref[320:448, :]
    zeros64 = jnp.zeros((HEAD_DIM, TM), MXU_DTYPE)

    def place_heads(ref, heads):
        for h, q in enumerate(heads):
            r0 = HEAD_DIM * (h // 2)
            ref[0, 0, r0:r0 + HEAD_DIM, h * TM:(h + 1) * TM] = q
            z0 = HEAD_DIM - r0
            ref[0, 0, z0:z0 + HEAD_DIM, h * TM:(h + 1) * TM] = zeros64

    q_heads = []
    for h in range(4):
        q = hf[64 * h:64 * h + 64]
        q = q * _rms_scale(q, 0) * gq_col_ref[...]
        q_heads.append((_rope_rows(q, fc64, fs64, 16) * (GQA_SCALE * LOG2E)).astype(MXU_DTYPE))
    place_heads(qa_ref, q_heads)
    vta_ref[0] = hf[256:384].astype(MXU_DTYPE)

    cq = hf[384:640]
    cq_n = (cq * _rms_scale(cq, 0) * gmq_col_ref[...]).astype(MXU_DTYPE)
    qm = _dot(wuq_ref[...], cq_n)
    for h in range(MLA_HEADS):
        qh = qm[128 * h:128 * h + 128]
        qm_ref[0, 128 * h:128 * h + 128, :] = (_rope_rows(qh, fcm, fsm, 8) * (MLA_SCALE * LOG2E)).astype(MXU_DTYPE)
    ckv_f = hf[640:768]
    ckv_fn = (ckv_f * _rms_scale(ckv_f, 0) * gkv_col_ref[...]).astype(MXU_DTYPE)
    vtm_ref[0] = _dot(wv_ref[...], ckv_fn).astype(MXU_DTYPE)

    zeros32 = jnp.zeros((DIFF_QK, TM), MXU_DTYPE)
    for c in range(2):
        for rb in range(4):
            r0 = 768 + 128 * c + 32 * rb
            piece = (_rope_rows(hf[r0:r0 + 32], fc32, fs32, 8) * (DIFF_SCALE * LOG2E)).astype(MXU_DTYPE)
            for cb in range(4):
                qd_ref[0, 0, c, 32 * rb:32 * rb + 32, cb * TM:(cb + 1) * TM] = piece if cb == rb else zeros32
    vtd_ref[0] = hf[1024:1280].astype(MXU_DTYPE)

    s_heads = []
    for h in range(4):
        q = hf[1280 + 64 * h:1344 + 64 * h]
        s_heads.append((_rope_rows(q, fc64, fs64, 16) * (GQA_SCALE * LOG2E)).astype(MXU_DTYPE))
    place_heads(qs_ref, s_heads)
    vts_ref[0] = hf[1536:1664].astype(MXU_DTYPE)


def _inproj(xs, modsel, g_pre, w, rope_t, rope_f, *, n_ctx_tiles):
    bsz, t_all, d = xs.shape
    nt = t_all // TM
    bt = lambda t, b: (b, t, 0)
    x_spec = pl.BlockSpec((1, TM, d), bt)
    mod_spec = pl.BlockSpec((1, 1, 9, d), lambda t, b: (b, jnp.where(t < n_ctx_tiles, 0, 1), 0, 0))
    tok = lambda n: (pl.BlockSpec((1, TM, n), bt), jax.ShapeDtypeStruct((bsz, t_all, n), MXU_DTYPE))
    feat = lambda n: (pl.BlockSpec((1, n, TM), lambda t, b: (b, 0, t)),
                      jax.ShapeDtypeStruct((bsz, n, t_all), MXU_DTYPE))
    qblk = (pl.BlockSpec((1, 1, 128, 4 * TM), lambda t, b: (b, t, 0, 0)),
            jax.ShapeDtypeStruct((bsz, nt, 128, 4 * TM), MXU_DTYPE))
    qdblk = (pl.BlockSpec((1, 1, 2, 128, 4 * TM), lambda t, b: (b, t, 0, 0, 0)),
             jax.ShapeDtypeStruct((bsz, nt, 2, 128, 4 * TM), MXU_DTYPE))
    outs = [tok(128), tok(512), tok(256), tok(128), tok(GATE_COLS),
            qblk, feat(128), feat(512), feat(256), qdblk, feat(256), qblk, feat(128)]
    consts = [g_pre, w["w_tok"], w["w_feat_t"], w["w_uq_t"], w["w_k_exp"], w["w_v_t"],
              w["gk_row"], w["gkv_row"], w["gq_col"], w["gmq_col"], w["gkv_col"]]
    return pl.pallas_call(
        _inproj_kernel,
        grid=(nt, bsz),
        in_specs=[x_spec, mod_spec] + [_const_spec(c.shape) for c in consts]
        + [pl.BlockSpec((4, TM, 128), lambda t, b: (0, t, 0)),
           pl.BlockSpec((448, TM), lambda t, b: (0, t))],
        out_specs=[o[0] for o in outs],
        out_shape=[o[1] for o in outs],
        compiler_params=_cparams(2),
        name="mixer_inproj",
    )(xs, modsel, *consts, rope_t, rope_f)


def _value_lhs(vt, r0):
    ones = jnp.ones((ONES_ROWS, vt.shape[1]), MXU_DTYPE)
    return jnp.concatenate([vt[r0:r0 + HEAD_DIM, :], ones], axis=0)


def _dense_attn_kernel(*refs, units, tk, n_ctx, n_lat, diff):
    if diff:
        (q1_ref, q2_ref, k_ref, vt_ref, lam_ref, subln_ref, o_ref,
         sa_ref, sb_ref, cma_ref, cmb_ref, m_ref, acc_ref) = refs
    else:
        q1_ref, q2_ref, k_ref, vt_ref, o_ref, sa_ref, sb_ref, cma_ref, cmb_ref, m_ref, acc_ref = refs
    step = pl.program_id(1)
    n_tiles = n_lat // tk
    nu = len(units)
    q_refs = (q1_ref, q2_ref)
    both, second = (0, 1), (1,)
    m_ref[...] = jnp.full(m_ref.shape, NEG_INF, F32)
    acc_ref[...] = jnp.zeros(acc_ref.shape, F32)
    buf_a, buf_b = (sa_ref, cma_ref), (sb_ref, cmb_ref)

    col_blocks = [(qi, u, c0, r0) for qi in both for u, (_, _, blocks) in enumerate(units) for c0, r0 in blocks]

    def scores_block(buf, off, size, qi, u, c0):
        s_ref, cm_ref = buf
        q_idx, k0, _ = units[u]
        s = _dot(k_ref[0, pl.ds(off, size), k0:k0 + 128], q_refs[qi][q_idx][:, c0:c0 + TM])
        s_ref[qi * nu + u, 0:size, c0:c0 + TM] = s
        cm_ref[qi * nu + u, :, c0:c0 + TM] = jnp.max(s, axis=0, keepdims=True)

    def consume_block(buf, off, size, qi, u, c0, r0):
        s_ref, cm_ref = buf
        w = qi * nu + u
        m_old = m_ref[w, :, c0:c0 + TM]
        m_new = jnp.maximum(m_old, cm_ref[w, :, c0:c0 + TM])
        alpha = jnp.exp2(m_old - m_new)
        m_ref[w, :, c0:c0 + TM] = m_new
        p = jnp.exp2(s_ref[w, 0:size, c0:c0 + TM] - m_new).astype(MXU_DTYPE)
        pv = _dot(_value_lhs(vt_ref[0, :, pl.ds(off, size)], r0), p)
        acc_ref[w, :, c0:c0 + TM] = alpha * acc_ref[w, :, c0:c0 + TM] + pv

    def scores(buf, off, size, halves):
        for qi, u, c0, _ in col_blocks:
            if qi in halves:
                scores_block(buf, off, size, qi, u, c0)

    def consume(buf, off, size, halves):
        for qi, u, c0, r0 in col_blocks:
            if qi in halves:
                consume_block(buf, off, size, qi, u, c0, r0)

    def finalize(halves):
        if diff:
            lf = lam_ref[...]
            lam_init = lam_ref[4:5, 0:1]
            lam = (jnp.exp(jnp.sum(lf[0:1, 0:DIFF_QK] * lf[1:2, 0:DIFF_QK], axis=1, keepdims=True))
                   - jnp.exp(jnp.sum(lf[2:3, 0:DIFF_QK] * lf[3:4, 0:DIFF_QK], axis=1, keepdims=True)) + lam_init)
        for qi in halves:
            out0 = qi * TM
            for u, (_, _, blocks) in enumerate(units):
                w = qi * nu + u
                inv = 1.0 / acc_ref[w, HEAD_DIM:HEAD_DIM + 1, :]
                if diff:
                    for hh in range(2):
                        c1, c2 = 2 * hh * TM, (2 * hh + 1) * TM
                        o = (acc_ref[w, 0:HEAD_DIM, c1:c1 + TM] * inv[:, c1:c1 + TM]
                             - lam * (acc_ref[w, 0:HEAD_DIM, c2:c2 + TM] * inv[:, c2:c2 + TM]))
                        o = o * _rms_scale(o, 0) * subln_ref[...] * (1.0 - lam_init)
                        h = 2 * u + hh
                        o_ref[0, HEAD_DIM * h:HEAD_DIM * h + HEAD_DIM, out0:out0 + TM] = o
                else:
                    for c0, _ in blocks:
                        h = (u * len(blocks) * TM + c0) // TM
                        o_ref[0, HEAD_DIM * h:HEAD_DIM * h + HEAD_DIM, out0:out0 + TM] = (
                            acc_ref[w, 0:HEAD_DIM, c0:c0 + TM] * inv[:, c0:c0 + TM])

    def lat(j):
        return n_ctx + j * tk if isinstance(j, int) else pl.multiple_of(n_ctx + j * tk, 128)

    @pl.when(step == 0)
    def _():
        scores(buf_a, 0, n_ctx, second)
        consume(buf_a, 0, n_ctx, second)
        o_ref[0, :, 0:TM] = jnp.zeros((BRANCH_W, TM), F32)
        finalize(second)

    @pl.when(step != 0)
    def _():
        buf_of = lambda parity: buf_a if parity % 2 else buf_b

        def stage(parity, j):
            for qi, u, c0, r0 in col_blocks:
                scores_block(buf_of(parity + 1), lat(j + 1), tk, qi, u, c0)
                consume_block(buf_of(parity), lat(j), tk, qi, u, c0, r0)

        scores(buf_a, 0, n_ctx, both)
        for qi, u, c0, r0 in col_blocks:
            scores_block(buf_b, n_ctx, tk, qi, u, c0)
            consume_block(buf_a, 0, n_ctx, qi, u, c0, r0)
        n_loop = (n_tiles - 1) // STAGES_PER_TRIP

        def body(i, carry):
            for r in range(STAGES_PER_TRIP):
                stage(r, STAGES_PER_TRIP * i + r)
            return carry

        lax.fori_loop(0, n_loop, body, 0)
        for j in range(STAGES_PER_TRIP * n_loop, n_tiles - 1):
            stage(j, j)
        consume(buf_of(n_tiles - 1), lat(n_tiles - 1), tk, both)
        finalize(both)


def _dense_attention(q, k, vt, *, kind, n_ctx, lam_tab=None, subln_col=None):
    bsz, t_all, dk = k.shape
    dv = vt.shape[1]
    n_lat = t_all - n_ctx
    n_steps = 1 + (n_lat // TM) // 2
    tiles = (lambda t: jnp.maximum(2 * t - 1, 0), lambda t: 2 * t)
    if kind == "gqa":
        q_specs = [pl.BlockSpec((1, 1, 128, 4 * TM), lambda b, t, f=f: (b, f(t), 0, 0)) for f in tiles]
        units = (((0, 0), 0, tuple((h * TM, HEAD_DIM * (h // 2)) for h in range(4))),)
        ncols = 4 * TM
    elif kind == "diff":
        q_specs = [pl.BlockSpec((1, 1, 2, 128, 4 * TM), lambda b, t, f=f: (b, f(t), 0, 0, 0)) for f in tiles]
        units = tuple(((0, 0, u), 128 * u,
                       tuple(((2 * hh + mm) * TM, HEAD_DIM * (2 * u + hh)) for hh in range(2) for mm in range(2)))
                      for u in range(2))
        ncols = 4 * TM
    else:
        q_specs = [pl.BlockSpec((1, 512, TM), lambda b, t, f=f: (b, 0, f(t))) for f in tiles]
        units = tuple(((0, slice(128 * h, 128 * h + 128)), 128 * h, ((0, HEAD_DIM * h),)) for h in range(4))
        ncols = TM
    in_specs = q_specs + [pl.BlockSpec((1, t_all, dk), lambda b, t: (b, 0, 0)),
                          pl.BlockSpec((1, dv, t_all), lambda b, t: (b, 0, 0))]
    args = [q, q, k, vt]
    if kind == "diff":
        in_specs += [_const_spec(lam_tab.shape), _const_spec(subln_col.shape)]
        args += [lam_tab, subln_col]
    nw = 2 * len(units)
    return pl.pallas_call(
        functools.partial(_dense_attn_kernel, units=units, tk=TK, n_ctx=n_ctx, n_lat=n_lat,
                          diff=(kind == "diff")),
        grid=(bsz, n_steps),
        in_specs=in_specs,
        out_specs=pl.BlockSpec((1, BRANCH_W, 2 * TM), lambda b, t: (b, 0, t)),
        out_shape=jax.ShapeDtypeStruct((bsz, BRANCH_W, 2 * TM * n_steps), F32),
        scratch_shapes=[pltpu.VMEM((nw, TK, ncols), F32), pltpu.VMEM((nw, TK, ncols), F32),
                        pltpu.VMEM((nw, 1, ncols), F32), pltpu.VMEM((nw, 1, ncols), F32),
                        pltpu.VMEM((nw, 1, ncols), F32), pltpu.VMEM((nw, ACC_ROWS, ncols), F32)],
        compiler_params=_cparams(2),
        name=f"attn_{kind}",
    )(*args)


def _window_attn_kernel(q_ref, k_ref, vt_ref, sink_ref, o_ref, *, n_ctx, t_all, wk):
    qt = pl.program_id(1)
    q0 = qt * TM
    start = pl.multiple_of(jnp.clip(q0 - WINDOW, n_ctx, t_all - wk), 128)
    kpos = start + lax.broadcasted_iota(jnp.int32, (wk, TM), 0)
    qpos = q0 + lax.broadcasted_iota(jnp.int32, (wk, TM), 1)
    allowed = (jnp.abs(kpos - qpos) <= WINDOW) & (qpos >= n_ctx)
    k_ctx, k_win = k_ref[0, 0:n_ctx, :], k_ref[0, pl.ds(start, wk), :]
    vt_ctx, vt_win = vt_ref[0, :, 0:n_ctx], vt_ref[0, :, pl.ds(start, wk)]
    heads = range(4)
    s_all = [(_dot(k_ctx, q_ref[0, 0, :, h * TM:(h + 1) * TM]),
              jnp.where(allowed, _dot(k_win, q_ref[0, 0, :, h * TM:(h + 1) * TM]), NEG_INF)) for h in heads]
    for h in heads:
        c0, r0 = h * TM, HEAD_DIM * (h // 2)
        s_ctx, s_win = s_all[h]
        sink = sink_ref[:, c0:c0 + TM] * LOG2E
        m = jnp.maximum(jnp.maximum(jnp.max(s_ctx, axis=0, keepdims=True),
                                    jnp.max(s_win, axis=0, keepdims=True)), sink)
        acc = (_dot(_value_lhs(vt_ctx, r0), jnp.exp2(s_ctx - m).astype(MXU_DTYPE))
               + _dot(_value_lhs(vt_win, r0), jnp.exp2(s_win - m).astype(MXU_DTYPE)))
        denom = acc[HEAD_DIM:HEAD_DIM + 1, :] + jnp.exp2(sink - m)
        o_ref[0, HEAD_DIM * h:HEAD_DIM * h + HEAD_DIM, :] = acc[0:HEAD_DIM, :] * (1.0 / denom)


def _window_attention(q, k, vt, sink_row, *, n_ctx):
    bsz, t_all, dk = k.shape
    nt = t_all // TM
    wk = TM + 2 * WINDOW
    return pl.pallas_call(
        functools.partial(_window_attn_kernel, n_ctx=n_ctx, t_all=t_all, wk=wk),
        grid=(bsz, nt),
        in_specs=[pl.BlockSpec((1, 1, 128, 4 * TM), lambda b, t: (b, t, 0, 0)),
                  pl.BlockSpec((1, t_all, dk), lambda b, t: (b, 0, 0)),
                  pl.BlockSpec((1, vt.shape[1], t_all), lambda b, t: (b, 0, 0)),
                  _const_spec(sink_row.shape)],
        out_specs=pl.BlockSpec((1, BRANCH_W, TM), lambda b, t: (b, 0, t)),
        out_shape=jax.ShapeDtypeStruct((bsz, BRANCH_W, t_all), F32),
        compiler_params=_cparams(2),
        name="attn_window",
    )(q, k, vt, sink_row)


def _merge_ffn_kernel(x_ref, mod_ref, gpost1_ref, oa_ref, om_ref, od_ref, os_ref, g_ref, wb_ref, wo_ref,
                      gpre2_ref, gpost2_ref, wg_ref, wu_ref, wd_ref, o_ref):
    x = x_ref[0]
    gate = mod_ref[0, 0, 5:6, :]
    y = None
    for i, o_t in enumerate((oa_ref, om_ref, od_ref, os_ref)):
        o_tok = o_t[0].T.astype(MXU_DTYPE)
        term = g_ref[0, :, i * D_MODEL:(i + 1) * D_MODEL].astype(F32) * _dot(o_tok, wb_ref[i])
        y = term if y is None else y + term
    z = _dot(y.astype(MXU_DTYPE), wo_ref[...])
    x = x + gate * (z * _rms_scale(z, -1) * gpost1_ref[...])
    o_ref[0] = _ffn_residual(x, mod_ref, gpre2_ref, gpost2_ref, wg_ref, wu_ref, wd_ref, 2)


def _merge_ffn(xs, modsel, g_post1, outs, gates, w_branch, w_out, g_pre2, g_post2, wg, wu, wd,
               *, n_ctx_tiles, latent_out):
    bsz, t_all, d = xs.shape
    x_spec, mod_spec = _token_specs(n_ctx_tiles)
    o_dense = pl.BlockSpec((1, BRANCH_W, TM), lambda b, t: (b, 0, t + 1))
    o_window = pl.BlockSpec((1, BRANCH_W, TM), lambda b, t: (b, 0, t))
    consts = [g_pre2, g_post2, wg, wu, wd]
    n_lat = t_all - n_ctx_tiles * TM
    return pl.pallas_call(
        _merge_ffn_kernel,
        grid=(bsz, t_all // TM),
        in_specs=[x_spec, mod_spec, _const_spec((1, d)), o_dense, o_dense, o_dense, o_window,
                  pl.BlockSpec((1, TM, GATE_COLS), lambda b, t: (b, t, 0)),
                  _const_spec(w_branch.shape), _const_spec(w_out.shape)] + [_const_spec(c.shape) for c in consts],
        out_specs=_latent_spec(n_ctx_tiles) if latent_out else x_spec,
        out_shape=jax.ShapeDtypeStruct((bsz, n_lat if latent_out else t_all, d), F32),
        compiler_params=_cparams(2),
        name="merge_ffn_half_2",
    )(xs, modsel, g_post1, *outs, gates, w_branch, w_out, *consts)


def _rope_tables(n_ctx, n_lat):
    pos = jnp.arange(n_lat)
    row = (pos // GRID_W).astype(F32)
    col = (pos % GRID_W).astype(F32)

    def lane_pattern(rot_dim):
        half = rot_dim // 2
        inv_freq = ROPE_THETA ** (-jnp.arange(0, half, 2, dtype=F32) / half)
        ar, ac = row[:, None] * inv_freq[None, :], col[:, None] * inv_freq[None, :]
        cos = jnp.concatenate([jnp.cos(ar), jnp.cos(ar), jnp.cos(ac), jnp.cos(ac)], axis=1)
        sin = jnp.concatenate([-jnp.sin(ar), jnp.sin(ar), -jnp.sin(ac), jnp.sin(ac)], axis=1)
        cos = jnp.concatenate([jnp.ones((n_ctx, rot_dim), F32), cos], axis=0)
        sin = jnp.concatenate([jnp.zeros((n_ctx, rot_dim), F32), sin], axis=0)
        return cos, sin

    c64, s64 = lane_pattern(HEAD_DIM)
    c32, s32 = lane_pattern(DIFF_QK)
    t_all = n_ctx + n_lat
    rope_t = jnp.stack([jnp.tile(c64, (1, 2)), jnp.tile(s64, (1, 2)), jnp.tile(c32, (1, 4)), jnp.tile(s32, (1, 4))])
    ones, zeros = jnp.ones((MLA_NOPE, t_all), F32), jnp.zeros((MLA_NOPE, t_all), F32)
    pad1, pad0 = jnp.ones((32, t_all), F32), jnp.zeros((32, t_all), F32)
    rope_f = jnp.concatenate([c64.T, s64.T, c32.T, s32.T,
                              ones, c32.T, pad1, zeros, s32.T, pad0], axis=0)
    return rope_t, rope_f


def _layer_weights(l, w_in, gqa_q_norm, gqa_k_norm, mla_q_norm, mla_kv_norm, mla_w_uq, mla_w_ukv):
    w = w_in[l]
    col = lambda a, n: w[:, a:a + n]
    kpe = jnp.zeros((D_MODEL, 128), F32).at[:, 64:96].set(col(M0 + 384, 32))
    w_tok = jnp.concatenate([col(A0 + 256, 128), col(M0 + 256, 128), kpe, col(X0 + 256, 256),
                             col(S0 + 256, 128), col(G0, GATE_COLS)], axis=1)
    w_feat = jnp.concatenate([col(A0, 256), col(A0 + 384, 128), col(M0, 256), col(M0 + 256, 128),
                              col(X0, 256), col(X0 + 512, 256), col(S0, 256), col(S0 + 384, 128)], axis=1)
    uq = mla_w_uq[l].reshape(MLA_Q_RANK, MLA_HEADS, MLA_NOPE + MLA_ROPE)
    uq = jnp.pad(uq, ((0, 0), (0, 0), (0, 128 - MLA_NOPE - MLA_ROPE))).reshape(MLA_Q_RANK, 512)
    ukv = mla_w_ukv[l].reshape(MLA_KV_RANK, MLA_HEADS, 2 * MLA_NOPE)
    k_exp = jnp.pad(ukv[:, :, :MLA_NOPE], ((0, 0), (0, 0), (0, 64))).reshape(MLA_KV_RANK, 512)
    v_t = ukv[:, :, MLA_NOPE:].reshape(MLA_KV_RANK, 256).T
    return {
        "w_tok": w_tok.astype(MXU_DTYPE), "w_feat_t": w_feat.T.astype(MXU_DTYPE),
        "w_uq_t": uq.T.astype(MXU_DTYPE), "w_k_exp": k_exp.astype(MXU_DTYPE), "w_v_t": v_t.astype(MXU_DTYPE),
        "gk_row": jnp.tile(gqa_k_norm[l], 2)[None, :], "gkv_row": mla_kv_norm[l][None, :],
        "gq_col": gqa_q_norm[l][:, None], "gmq_col": mla_q_norm[l][:, None], "gkv_col": mla_kv_norm[l][:, None],
    }


def kernel(x, c, ctx, c_ctx, w_mod, b_mod, g_pre, g_post, w_ffn_gate, w_ffn_up, w_ffn_down, w_in,
           gqa_q_norm, gqa_k_norm, mla_q_norm, mla_kv_norm, mla_w_uq, mla_w_ukv,
           diff_lambda, diff_subln, swa_sink, w_branch, w_out):
    bsz, n_lat, d = x.shape
    n_ctx = ctx.shape[1]
    assert d == D_MODEL and n_ctx == TM and n_lat % (2 * TK) == 0 and bsz + 1 <= MOD_ROWS
    n_ctx_tiles = n_ctx // TM

    cvec = jnp.zeros((MOD_ROWS, d), F32).at[:bsz].set(c).at[bsz].set(c_ctx)
    mod = _modulation(cvec, w_mod, b_mod).reshape(DEPTH, MOD_ROWS, 9, d)
    modsel = jnp.stack([jnp.broadcast_to(mod[:, bsz:bsz + 1], (DEPTH, bsz, 9, d)), mod[:, :bsz]], axis=2)

    rope_t, rope_f = _rope_tables(n_ctx, n_lat)
    xs = (ctx, x)
    cast = lambda a: a.astype(MXU_DTYPE)

    for l in range(DEPTH):
        ffn_w = lambda i: (cast(w_ffn_gate[l, i]), cast(w_ffn_up[l, i]), cast(w_ffn_down[l, i]))
        xs = _ffn_half(xs, modsel[l], g_pre[l, 0][None], g_post[l, 0][None], *ffn_w(0),
                       sub=0, n_ctx_tiles=n_ctx_tiles)
        w = _layer_weights(l, w_in, gqa_q_norm, gqa_k_norm, mla_q_norm, mla_kv_norm, mla_w_uq, mla_w_ukv)
        (k_a, k_m, k_d, k_s, gates, q_a, vt_a, q_m, vt_m, q_d, vt_d, q_s, vt_s) = _inproj(
            xs, modsel[l], g_pre[l, 1][None], w, rope_t, rope_f, n_ctx_tiles=n_ctx_tiles)
        lam_init = 0.8 - 0.6 * math.exp(-0.3 * l)
        lam_tab = jnp.zeros((8, 128), F32).at[0:4, 0:DIFF_QK].set(diff_lambda[l]).at[4, :].set(lam_init)
        sink_row = jnp.repeat(swa_sink[l], TM)[None, :]
        outs = (_dense_attention(q_a, k_a, vt_a, kind="gqa", n_ctx=n_ctx),
                _dense_attention(q_m, k_m, vt_m, kind="mla", n_ctx=n_ctx),
                _dense_attention(q_d, k_d, vt_d, kind="diff", n_ctx=n_ctx,
                                 lam_tab=lam_tab, subln_col=diff_subln[l][:, None]),
                _window_attention(q_s, k_s, vt_s, sink_row, n_ctx=n_ctx))
        xs = _merge_ffn(xs, modsel[l], g_post[l, 1][None], outs, gates, cast(w_branch[l]), cast(w_out[l]),
                        g_pre[l, 2][None], g_post[l, 2][None], *ffn_w(1), n_ctx_tiles=n_ctx_tiles,
                        latent_out=(l == DEPTH - 1))
    return xs
```

```python
import functools
import math

import jax
import jax.numpy as jnp
from jax import lax
from jax.experimental import pallas as pl
from jax.experimental.pallas import tpu as pltpu

MXU_DTYPE = jnp.bfloat16
F32 = jnp.float32

D_MODEL = 1024
D_FF = 2816
DEPTH = 2
GRID_W = 64
WINDOW = 128
ROPE_THETA = 10000.0
EPS = 1e-6
NEG_INF = -1e30
HEAD_DIM = 64
MLA_HEADS = 4
MLA_Q_RANK = 256
MLA_KV_RANK = 128
MLA_NOPE = 64
MLA_ROPE = 32
DIFF_QK = 32
N_BRANCH = 4
BRANCH_W = 256
GQA_SCALE = HEAD_DIM ** -0.5
MLA_SCALE = (MLA_NOPE + MLA_ROPE) ** -0.5
DIFF_SCALE = DIFF_QK ** -0.5
LOG2E = math.log2(math.e)
ONES_ROWS = 16
ACC_ROWS = HEAD_DIM + ONES_ROWS

A0, M0, X0, S0, G0 = 0, 512, 928, 1696, 2208
GATE_COLS = N_BRANCH * D_MODEL
TOK_COLS = 768
FEAT_ROWS = 1664

TM = 256
SPS = 2
TK = 512
STAGES_PER_TRIP = 4
MOD_ROWS = 16
MOD_TN = 1024
V7X_VMEM_LIMIT = 56 * 1024 * 1024


def _cparams(n_axes):
    return pltpu.CompilerParams(dimension_semantics=("arbitrary",) * n_axes,
                                vmem_limit_bytes=V7X_VMEM_LIMIT)


def _const_spec(shape):
    n = len(shape)
    return pl.BlockSpec(shape, lambda *_: (0,) * n, pipeline_mode=pl.Buffered(1))


def _per_sample(body, batched):
    def kernel(*refs):
        for i in range(SPS):
            body(*[r.at[i:i + 1] if b else r for r, b in zip(refs, batched)])
    return kernel


def _sigmoid(v):
    return 1.0 / (1.0 + jnp.exp(-v))


def _dot(a, b):
    return jnp.dot(a, b, preferred_element_type=F32)


def _rms_scale(v, axis):
    return lax.rsqrt(jnp.mean(v * v, axis=axis, keepdims=True) + EPS)


def _mod_kernel(c_ref, w_ref, b_ref, o_ref):
    c = c_ref[...]
    s = (c * _sigmoid(c)).astype(MXU_DTYPE)
    o_ref[0] = _dot(s, w_ref[0].astype(MXU_DTYPE)) + b_ref[0]


def _modulation(cvec, w_mod, b_mod):
    depth, d, n = w_mod.shape
    return pl.pallas_call(
        _mod_kernel,
        grid=(depth, n // MOD_TN),
        in_specs=[pl.BlockSpec((MOD_ROWS, d), lambda l, j: (0, 0)),
                  pl.BlockSpec((1, d, MOD_TN), lambda l, j: (l, 0, j)),
                  pl.BlockSpec((1, 1, MOD_TN), lambda l, j: (l, 0, j))],
        out_specs=pl.BlockSpec((1, MOD_ROWS, MOD_TN), lambda l, j: (l, 0, j)),
        out_shape=jax.ShapeDtypeStruct((depth, MOD_ROWS, n), F32),
        compiler_params=_cparams(2),
        name="modulation",
    )(cvec, w_mod, b_mod.reshape(depth, 1, n))


def _adaln(x, g_row, shift, scale):
    return (x * _rms_scale(x, -1) * g_row) * (1.0 + scale) + shift


def _ffn_residual(x, mod_ref, gpre_ref, gpost_ref, wg_ref, wu_ref, wd_ref, sub):
    shift = mod_ref[0, 0, 3 * sub:3 * sub + 1, :]
    scale = mod_ref[0, 0, 3 * sub + 1:3 * sub + 2, :]
    gate = mod_ref[0, 0, 3 * sub + 2:3 * sub + 3, :]
    u = _adaln(x, gpre_ref[...], shift, scale).astype(MXU_DTYPE)
    a = _dot(u, wg_ref[...])
    b = _dot(u, wu_ref[...])
    h = ((a * _sigmoid(a)) * b).astype(MXU_DTYPE)
    y = _dot(h, wd_ref[...])
    yn = y * _rms_scale(y, -1) * gpost_ref[...]
    return x + 0.5 * gate * yn


def _ffn_kernel(*refs, sub, n_ctx_tiles, split_in):
    if split_in:
        ctx_ref, lat_ref, mod_ref, gpre_ref, gpost_ref, wg_ref, wu_ref, wd_ref, o_ref = refs
        x = jnp.where(pl.program_id(1) < n_ctx_tiles, ctx_ref[0], lat_ref[0])
    else:
        x_ref, mod_ref, gpre_ref, gpost_ref, wg_ref, wu_ref, wd_ref, o_ref = refs
        x = x_ref[0]
    o_ref[0] = _ffn_residual(x, mod_ref, gpre_ref, gpost_ref, wg_ref, wu_ref, wd_ref, sub)


def _token_specs(n_ctx_tiles):
    x_spec = pl.BlockSpec((SPS, TM, D_MODEL), lambda b, t: (b, t, 0))
    mod_spec = pl.BlockSpec((SPS, 1, 9, D_MODEL),
                            lambda b, t: (b, jnp.where(t < n_ctx_tiles, 0, 1), 0, 0))
    return x_spec, mod_spec


def _latent_spec(n_ctx_tiles):
    return pl.BlockSpec((SPS, TM, D_MODEL), lambda b, t: (b, jnp.maximum(t - n_ctx_tiles, 0), 0))


def _ffn_half(xs, modsel, g_pre, g_post, wg, wu, wd, *, sub, n_ctx_tiles):
    split_in = isinstance(xs, tuple)
    x_spec, mod_spec = _token_specs(n_ctx_tiles)
    if split_in:
        ctx, lat = xs
        bsz, t_all, d = lat.shape[0], ctx.shape[1] + lat.shape[1], lat.shape[2]
        x_specs = [pl.BlockSpec((SPS, TM, d), lambda b, t: (b, jnp.minimum(t, n_ctx_tiles - 1), 0)),
                   _latent_spec(n_ctx_tiles)]
        x_args = [ctx, lat]
    else:
        bsz, t_all, d = xs.shape
        x_specs, x_args = [x_spec], [xs]
    batched = (True,) * (len(x_args) + 1) + (False,) * 5 + (True,)
    return pl.pallas_call(
        _per_sample(functools.partial(_ffn_kernel, sub=sub, n_ctx_tiles=n_ctx_tiles, split_in=split_in), batched),
        grid=(bsz // SPS, t_all // TM),
        in_specs=x_specs + [mod_spec, _const_spec((1, d)), _const_spec((1, d)),
                            _const_spec(wg.shape), _const_spec(wu.shape), _const_spec(wd.shape)],
        out_specs=x_spec,
        out_shape=jax.ShapeDtypeStruct((bsz, t_all, d), F32),
        compiler_params=_cparams(2),
        name=f"ffn_half_{sub}",
    )(*x_args, modsel, g_pre, g_post, wg, wu, wd)


def _rope_lanes(v, cos, sin, pair):
    width = v.shape[1]
    lane = lax.broadcasted_iota(jnp.int32, v.shape, 1)
    first = (lane & pair) == 0
    partner = jnp.where(first, pltpu.roll(v, width - pair, 1), pltpu.roll(v, pair, 1))
    return v * cos + partner * sin


def _swap_row_blocks(v, pair):
    parts = []
    for r in range(0, v.shape[0], 2 * pair):
        parts.append(v[r + pair:r + 2 * pair])
        parts.append(v[r:r + pair])
    return jnp.concatenate(parts, axis=0)


def _rope_rows(v, cos, sin, pair):
    return v * cos + _swap_row_blocks(v, pair) * sin


def _inproj_kernel(x_ref, mod_ref, gpre_ref, wtok_ref, wft_ref, wuq_ref, wkx_ref, wv_ref,
                   gk_ref, gkv_row_ref, gq_col_ref, gmq_col_ref, gkv_col_ref,
                   rope_t_ref, rope_f_ref,
                   ka_ref, km_ref, kd_ref, ks_ref, g_ref,
                   qa_ref, vta_ref, qm_ref, vtm_ref, qd_ref, vtd_ref, qs_ref, vts_ref):
    x = x_ref[0]
    shift = mod_ref[0, 0, 3:4, :]
    scale = mod_ref[0, 0, 4:5, :]
    u = _adaln(x, gpre_ref[...], shift, scale).astype(MXU_DTYPE)

    c64, s64, c32, s32 = rope_t_ref[0], rope_t_ref[1], rope_t_ref[2], rope_t_ref[3]

    ht = _dot(u, wtok_ref[:, 0:TOK_COLS])
    k_a = ht[:, 0:128]
    sq = k_a * k_a
    lane = lax.broadcasted_iota(jnp.int32, sq.shape, 1)
    low = lane < HEAD_DIM
    ss_lo = jnp.sum(jnp.where(low, sq, 0.0), axis=1, keepdims=True)
    ss_hi = jnp.sum(jnp.where(low, 0.0, sq), axis=1, keepdims=True)
    ms = jnp.where(low, ss_lo, ss_hi) * (1.0 / HEAD_DIM)
    k_a = k_a * lax.rsqrt(ms + EPS) * gk_ref[...]
    ka_ref[0] = _rope_lanes(k_a, c64, s64, 16).astype(MXU_DTYPE)

    ckv = ht[:, 128:256]
    ckv_n = (ckv * _rms_scale(ckv, -1) * gkv_row_ref[...]).astype(MXU_DTYPE)
    k_nope = _dot(ckv_n, wkx_ref[...])
    k_pe = _rope_lanes(ht[:, 256:384], c32, s32, 8)
    for h in range(MLA_HEADS):
        km_ref[0, :, 128 * h:128 * h + 128] = (k_nope[:, 128 * h:128 * h + 128] + k_pe).astype(MXU_DTYPE)

    for j in range(2):
        kd_ref[0, :, 128 * j:128 * j + 128] = _rope_lanes(
            ht[:, 384 + 128 * j:512 + 128 * j], c32, s32, 8).astype(MXU_DTYPE)
    ks_ref[0] = _rope_lanes(ht[:, 640:768], c64, s64, 16).astype(MXU_DTYPE)

    for j in range(GATE_COLS // D_MODEL):
        lo = TOK_COLS + j * D_MODEL
        hg = _dot(u, wtok_ref[:, lo:lo + D_MODEL])
        g_ref[0, :, j * D_MODEL:(j + 1) * D_MODEL] = _sigmoid(hg).astype(g_ref.dtype)

    hf = lax.dot_general(wft_ref[...], u, (((1,), (1,)), ((), ())), preferred_element_type=F32)
    fc64, fs64 = rope_f_ref[0:64, :], rope_f_ref[64:128, :]
    fc32, fs32 = rope_f_ref[128:160, :], rope_f_ref[160:192, :]
    fcm, fsm = rope_f_ref[192:320, :], rope_f_ref[320:448, :]
    zeros64 = jnp.zeros((HEAD_DIM, TM), MXU_DTYPE)

    def place_heads(ref, heads):
        for h, q in enumerate(heads):
            r0 = HEAD_DIM * (h // 2)
            ref[0, 0, r0:r0 + HEAD_DIM, h * TM:(h + 1) * TM] = q
            z0 = HEAD_DIM - r0
            ref[0, 0, z0:z0 + HEAD_DIM, h * TM:(h + 1) * TM] = zeros64

    q_heads = []
    for h in range(4):
        q = hf[64 * h:64 * h + 64]
        q = q * _rms_scale(q, 0) * gq_col_ref[...]
        q_heads.append((_rope_rows(q, fc64, fs64, 16) * (GQA_SCALE * LOG2E)).astype(MXU_DTYPE))
    place_heads(qa_ref, q_heads)
    vta_ref[0] = hf[256:384].astype(MXU_DTYPE)

    cq = hf[384:640]
    cq_n = (cq * _rms_scale(cq, 0) * gmq_col_ref[...]).astype(MXU_DTYPE)
    qm = _dot(wuq_ref[...], cq_n)
    for h in range(MLA_HEADS):
        qh = qm[128 * h:128 * h + 128]
        qm_ref[0, 128 * h:128 * h + 128, :] = (_rope_rows(qh, fcm, fsm, 8) * (MLA_SCALE * LOG2E)).astype(MXU_DTYPE)
    ckv_f = hf[640:768]
    ckv_fn = (ckv_f * _rms_scale(ckv_f, 0) * gkv_col_ref[...]).astype(MXU_DTYPE)
    vtm_ref[0] = _dot(wv_ref[...], ckv_fn).astype(MXU_DTYPE)

    zeros32 = jnp.zeros((DIFF_QK, TM), MXU_DTYPE)
    for c in range(2):
        for rb in range(4):
            r0 = 768 + 128 * c + 32 * rb
            piece = (_rope_rows(hf[r0:r0 + 32], fc32, fs32, 8) * (DIFF_SCALE * LOG2E)).astype(MXU_DTYPE)
            for cb in range(4):
                qd_ref[0, 0, c, 32 * rb:32 * rb + 32, cb * TM:(cb + 1) * TM] = piece if cb == rb else zeros32
    vtd_ref[0] = hf[1024:1280].astype(MXU_DTYPE)

    s_heads = []
    for h in range(4):
        q = hf[1280 + 64 * h:1344 + 64 * h]
        s_heads.append((_rope_rows(q, fc64, fs64, 16) * (GQA_SCALE * LOG2E)).astype(MXU_DTYPE))
    place_heads(qs_ref, s_heads)
    vts_ref[0] = hf[1536:1664].astype(MXU_DTYPE)


def _inproj(xs, modsel, g_pre, w, rope_t, rope_f, *, n_ctx_tiles):
    bsz, t_all, d = xs.shape
    nt = t_all // TM
    bt = lambda t, b: (b, t, 0)
    x_spec = pl.BlockSpec((SPS, TM, d), bt)
    mod_spec = pl.BlockSpec((SPS, 1, 9, d), lambda t, b: (b, jnp.where(t < n_ctx_tiles, 0, 1), 0, 0))
    tok = lambda n: (pl.BlockSpec((SPS, TM, n), bt), jax.ShapeDtypeStruct((bsz, t_all, n), MXU_DTYPE))
    feat = lambda n: (pl.BlockSpec((SPS, n, TM), lambda t, b: (b, 0, t)),
                      jax.ShapeDtypeStruct((bsz, n, t_all), MXU_DTYPE))
    qblk = (pl.BlockSpec((SPS, 1, 128, 4 * TM), lambda t, b: (b, t, 0, 0)),
            jax.ShapeDtypeStruct((bsz, nt, 128, 4 * TM), MXU_DTYPE))
    qdblk = (pl.BlockSpec((SPS, 1, 2, 128, 4 * TM), lambda t, b: (b, t, 0, 0, 0)),
             jax.ShapeDtypeStruct((bsz, nt, 2, 128, 4 * TM), MXU_DTYPE))
    outs = [tok(128), tok(512), tok(256), tok(128), tok(GATE_COLS),
            qblk, feat(128), feat(512), feat(256), qdblk, feat(256), qblk, feat(128)]
    consts = [g_pre, w["w_tok"], w["w_feat_t"], w["w_uq_t"], w["w_k_exp"], w["w_v_t"],
              w["gk_row"], w["gkv_row"], w["gq_col"], w["gmq_col"], w["gkv_col"]]
    batched = (True, True) + (False,) * (len(consts) + 2) + (True,) * len(outs)
    return pl.pallas_call(
        _per_sample(_inproj_kernel, batched),
        grid=(nt, bsz // SPS),
        in_specs=[x_spec, mod_spec] + [_const_spec(c.shape) for c in consts]
        + [pl.BlockSpec((4, TM, 128), lambda t, b: (0, t, 0)),
           pl.BlockSpec((448, TM), lambda t, b: (0, t))],
        out_specs=[o[0] for o in outs],
        out_shape=[o[1] for o in outs],
        compiler_params=_cparams(2),
        name="mixer_inproj",
    )(xs, modsel, *consts, rope_t, rope_f)


def _value_lhs(vt, r0):
    ones = jnp.ones((ONES_ROWS, vt.shape[1]), MXU_DTYPE)
    return jnp.concatenate([vt[r0:r0 + HEAD_DIM, :], ones], axis=0)


def _dense_attn_kernel(*refs, units, tk, n_ctx, n_lat, diff):
    if diff:
        (q1_ref, q2_ref, k_ref, vt_ref, lam_ref, subln_ref, o_ref,
         sa_ref, sb_ref, cma_ref, cmb_ref, m_ref, acc_ref) = refs
    else:
        q1_ref, q2_ref, k_ref, vt_ref, o_ref, sa_ref, sb_ref, cma_ref, cmb_ref, m_ref, acc_ref = refs
    step = pl.program_id(1)
    n_tiles = n_lat // tk
    nu = len(units)
    q_refs = (q1_ref, q2_ref)
    both, second = (0, 1), (1,)
    m_ref[...] = jnp.full(m_ref.shape, NEG_INF, F32)
    acc_ref[...] = jnp.zeros(acc_ref.shape, F32)
    buf_a, buf_b = (sa_ref, cma_ref), (sb_ref, cmb_ref)

    col_blocks = [(qi, u, c0, r0) for qi in both for u, (_, _, blocks) in enumerate(units) for c0, r0 in blocks]

    def scores_block(buf, off, size, qi, u, c0):
        s_ref, cm_ref = buf
        q_idx, k0, _ = units[u]
        s = _dot(k_ref[0, pl.ds(off, size), k0:k0 + 128], q_refs[qi][q_idx][:, c0:c0 + TM])
        s_ref[qi * nu + u, 0:size, c0:c0 + TM] = s
        cm_ref[qi * nu + u, :, c0:c0 + TM] = jnp.max(s, axis=0, keepdims=True)

    def consume_block(buf, off, size, qi, u, c0, r0):
        s_ref, cm_ref = buf
        w = qi * nu + u
        m_old = m_ref[w, :, c0:c0 + TM]
        m_new = jnp.maximum(m_old, cm_ref[w, :, c0:c0 + TM])
        alpha = jnp.exp2(m_old - m_new)
        m_ref[w, :, c0:c0 + TM] = m_new
        p = jnp.exp2(s_ref[w, 0:size, c0:c0 + TM] - m_new).astype(MXU_DTYPE)
        pv = _dot(_value_lhs(vt_ref[0, :, pl.ds(off, size)], r0), p)
        acc_ref[w, :, c0:c0 + TM] = alpha * acc_ref[w, :, c0:c0 + TM] + pv

    def scores(buf, off, size, halves):
        for qi, u, c0, _ in col_blocks:
            if qi in halves:
                scores_block(buf, off, size, qi, u, c0)

    def consume(buf, off, size, halves):
        for qi, u, c0, r0 in col_blocks:
            if qi in halves:
                consume_block(buf, off, size, qi, u, c0, r0)

    def finalize(halves):
        if diff:
            lf = lam_ref[...]
            lam_init = lam_ref[4:5, 0:1]
            lam = (jnp.exp(jnp.sum(lf[0:1, 0:DIFF_QK] * lf[1:2, 0:DIFF_QK], axis=1, keepdims=True))
                   - jnp.exp(jnp.sum(lf[2:3, 0:DIFF_QK] * lf[3:4, 0:DIFF_QK], axis=1, keepdims=True)) + lam_init)
        for qi in halves:
            out0 = qi * TM
            for u, (_, _, blocks) in enumerate(units):
                w = qi * nu + u
                inv = 1.0 / acc_ref[w, HEAD_DIM:HEAD_DIM + 1, :]
                if diff:
                    for hh in range(2):
                        c1, c2 = 2 * hh * TM, (2 * hh + 1) * TM
                        o = (acc_ref[w, 0:HEAD_DIM, c1:c1 + TM] * inv[:, c1:c1 + TM]
                             - lam * (acc_ref[w, 0:HEAD_DIM, c2:c2 + TM] * inv[:, c2:c2 + TM]))
                        o = o * _rms_scale(o, 0) * subln_ref[...] * (1.0 - lam_init)
                        h = 2 * u + hh
                        o_ref[0, HEAD_DIM * h:HEAD_DIM * h + HEAD_DIM, out0:out0 + TM] = o
                else:
                    for c0, _ in blocks:
                        h = (u * len(blocks) * TM + c0) // TM
                        o_ref[0, HEAD_DIM * h:HEAD_DIM * h + HEAD_DIM, out0:out0 + TM] = (
                            acc_ref[w, 0:HEAD_DIM, c0:c0 + TM] * inv[:, c0:c0 + TM])

    def lat(j):
        return n_ctx + j * tk if isinstance(j, int) else pl.multiple_of(n_ctx + j * tk, 128)

    @pl.when(step == 0)
    def _():
        scores(buf_a, 0, n_ctx, second)
        consume(buf_a, 0, n_ctx, second)
        o_ref[0, :, 0:TM] = jnp.zeros((BRANCH_W, TM), F32)
        finalize(second)

    @pl.when(step != 0)
    def _():
        buf_of = lambda parity: buf_a if parity % 2 else buf_b

        def stage(parity, j):
            for qi, u, c0, r0 in col_blocks:
                scores_block(buf_of(parity + 1), lat(j + 1), tk, qi, u, c0)
                consume_block(buf_of(parity), lat(j), tk, qi, u, c0, r0)

        scores(buf_a, 0, n_ctx, both)
        for qi, u, c0, r0 in col_blocks:
            scores_block(buf_b, n_ctx, tk, qi, u, c0)
            consume_block(buf_a, 0, n_ctx, qi, u, c0, r0)
        n_loop = (n_tiles - 1) // STAGES_PER_TRIP

        def body(i, carry):
            for r in range(STAGES_PER_TRIP):
                stage(r, STAGES_PER_TRIP * i + r)
            return carry

        lax.fori_loop(0, n_loop, body, 0)
        for j in range(STAGES_PER_TRIP * n_loop, n_tiles - 1):
            stage(j, j)
        consume(buf_of(n_tiles - 1), lat(n_tiles - 1), tk, both)
        finalize(both)


def _dense_attention(q, k, vt, *, kind, n_ctx, lam_tab=None, subln_col=None):
    bsz, t_all, dk = k.shape
    dv = vt.shape[1]
    n_lat = t_all - n_ctx
    n_steps = 1 + (n_lat // TM) // 2
    tiles = (lambda t: jnp.maximum(2 * t - 1, 0), lambda t: 2 * t)
    if kind == "gqa":
        q_specs = [pl.BlockSpec((1, 1, 128, 4 * TM), lambda b, t, f=f: (b, f(t), 0, 0)) for f in tiles]
        units = (((0, 0), 0, tuple((h * TM, HEAD_DIM * (h // 2)) for h in range(4))),)
        ncols = 4 * TM
    elif kind == "diff":
        q_specs = [pl.BlockSpec((1, 1, 2, 128, 4 * TM), lambda b, t, f=f: (b, f(t), 0, 0, 0)) for f in tiles]
        units = tuple(((0, 0, u), 128 * u,
                       tuple(((2 * hh + mm) * TM, HEAD_DIM * (2 * u + hh)) for hh in range(2) for mm in range(2)))
                      for u in range(2))
        ncols = 4 * TM
    else:
        q_specs = [pl.BlockSpec((1, 512, TM), lambda b, t, f=f: (b, 0, f(t))) for f in tiles]
        units = tuple(((0, slice(128 * h, 128 * h + 128)), 128 * h, ((0, HEAD_DIM * h),)) for h in range(4))
        ncols = TM
    in_specs = q_specs + [pl.BlockSpec((1, t_all, dk), lambda b, t: (b, 0, 0)),
                          pl.BlockSpec((1, dv, t_all), lambda b, t: (b, 0, 0))]
    args = [q, q, k, vt]
    if kind == "diff":
        in_specs += [_const_spec(lam_tab.shape), _const_spec(subln_col.shape)]
        args += [lam_tab, subln_col]
    nw = 2 * len(units)
    return pl.pallas_call(
        functools.partial(_dense_attn_kernel, units=units, tk=TK, n_ctx=n_ctx, n_lat=n_lat,
                          diff=(kind == "diff")),
        grid=(bsz, n_steps),
        in_specs=in_specs,
        out_specs=pl.BlockSpec((1, BRANCH_W, 2 * TM), lambda b, t: (b, 0, t)),
        out_shape=jax.ShapeDtypeStruct((bsz, BRANCH_W, 2 * TM * n_steps), F32),
        scratch_shapes=[pltpu.VMEM((nw, TK, ncols), F32), pltpu.VMEM((nw, TK, ncols), F32),
                        pltpu.VMEM((nw, 1, ncols), F32), pltpu.VMEM((nw, 1, ncols), F32),
                        pltpu.VMEM((nw, 1, ncols), F32), pltpu.VMEM((nw, ACC_ROWS, ncols), F32)],
        compiler_params=_cparams(2),
        name=f"attn_{kind}",
    )(*args)


def _window_attn_kernel(q_ref, k_ref, vt_ref, sink_ref, o_ref, *, n_ctx, t_all, wk):
    qt = pl.program_id(1)
    q0 = qt * TM
    start = pl.multiple_of(jnp.clip(q0 - WINDOW, n_ctx, t_all - wk), 128)
    kpos = start + lax.broadcasted_iota(jnp.int32, (wk, TM), 0)
    qpos = q0 + lax.broadcasted_iota(jnp.int32, (wk, TM), 1)
    allowed = (jnp.abs(kpos - qpos) <= WINDOW) & (qpos >= n_ctx)
    k_ctx, k_win = k_ref[0, 0:n_ctx, :], k_ref[0, pl.ds(start, wk), :]
    vt_ctx, vt_win = vt_ref[0, :, 0:n_ctx], vt_ref[0, :, pl.ds(start, wk)]
    heads = range(4)
    s_all = [(_dot(k_ctx, q_ref[0, 0, :, h * TM:(h + 1) * TM]),
              jnp.where(allowed, _dot(k_win, q_ref[0, 0, :, h * TM:(h + 1) * TM]), NEG_INF)) for h in heads]
    for h in heads:
        c0, r0 = h * TM, HEAD_DIM * (h // 2)
        s_ctx, s_win = s_all[h]
        sink = sink_ref[:, c0:c0 + TM] * LOG2E
        m = jnp.maximum(jnp.maximum(jnp.max(s_ctx, axis=0, keepdims=True),
                                    jnp.max(s_win, axis=0, keepdims=True)), sink)
        acc = (_dot(_value_lhs(vt_ctx, r0), jnp.exp2(s_ctx - m).astype(MXU_DTYPE))
               + _dot(_value_lhs(vt_win, r0), jnp.exp2(s_win - m).astype(MXU_DTYPE)))
        denom = acc[HEAD_DIM:HEAD_DIM + 1, :] + jnp.exp2(sink - m)
        o_ref[0, HEAD_DIM * h:HEAD_DIM * h + HEAD_DIM, :] = acc[0:HEAD_DIM, :] * (1.0 / denom)


def _window_attention(q, k, vt, sink_row, *, n_ctx):
    bsz, t_all, dk = k.shape
    nt = t_all // TM
    wk = TM + 2 * WINDOW
    return pl.pallas_call(
        functools.partial(_window_attn_kernel, n_ctx=n_ctx, t_all=t_all, wk=wk),
        grid=(bsz, nt),
        in_specs=[pl.BlockSpec((1, 1, 128, 4 * TM), lambda b, t: (b, t, 0, 0)),
                  pl.BlockSpec((1, t_all, dk), lambda b, t: (b, 0, 0)),
                  pl.BlockSpec((1, vt.shape[1], t_all), lambda b, t: (b, 0, 0)),
                  _const_spec(sink_row.shape)],
        out_specs=pl.BlockSpec((1, BRANCH_W, TM), lambda b, t: (b, 0, t)),
        out_shape=jax.ShapeDtypeStruct((bsz, BRANCH_W, t_all), F32),
        compiler_params=_cparams(2),
        name="attn_window",
    )(q, k, vt, sink_row)


def _merge_ffn_kernel(x_ref, mod_ref, gpost1_ref, oa_ref, om_ref, od_ref, os_ref, g_ref, wb_ref, wo_ref,
                      gpre2_ref, gpost2_ref, wg_ref, wu_ref, wd_ref, o_ref):
    x = x_ref[0]
    gate = mod_ref[0, 0, 5:6, :]
    y = None
    for i, o_t in enumerate((oa_ref, om_ref, od_ref, os_ref)):
        o_tok = o_t[0].T.astype(MXU_DTYPE)
        term = g_ref[0, :, i * D_MODEL:(i + 1) * D_MODEL].astype(F32) * _dot(o_tok, wb_ref[i])
        y = term if y is None else y + term
    z = _dot(y.astype(MXU_DTYPE), wo_ref[...])
    x = x + gate * (z * _rms_scale(z, -1) * gpost1_ref[...])
    o_ref[0] = _ffn_residual(x, mod_ref, gpre2_ref, gpost2_ref, wg_ref, wu_ref, wd_ref, 2)


def _merge_ffn(xs, modsel, g_post1, outs, gates, w_branch, w_out, g_pre2, g_post2, wg, wu, wd,
               *, n_ctx_tiles, latent_out):
    bsz, t_all, d = xs.shape
    x_spec, mod_spec = _token_specs(n_ctx_tiles)
    o_dense = pl.BlockSpec((SPS, BRANCH_W, TM), lambda b, t: (b, 0, t + 1))
    o_window = pl.BlockSpec((SPS, BRANCH_W, TM), lambda b, t: (b, 0, t))
    consts = [g_pre2, g_post2, wg, wu, wd]
    n_lat = t_all - n_ctx_tiles * TM
    batched = (True, True, False) + (True,) * 5 + (False,) * (2 + len(consts)) + (True,)
    return pl.pallas_call(
        _per_sample(_merge_ffn_kernel, batched),
        grid=(bsz // SPS, t_all // TM),
        in_specs=[x_spec, mod_spec, _const_spec((1, d)), o_dense, o_dense, o_dense, o_window,
                  pl.BlockSpec((SPS, TM, GATE_COLS), lambda b, t: (b, t, 0)),
                  _const_spec(w_branch.shape), _const_spec(w_out.shape)] + [_const_spec(c.shape) for c in consts],
        out_specs=_latent_spec(n_ctx_tiles) if latent_out else x_spec,
        out_shape=jax.ShapeDtypeStruct((bsz, n_lat if latent_out else t_all, d), F32),
        compiler_params=_cparams(2),
        name="merge_ffn_half_2",
    )(xs, modsel, g_post1, *outs, gates, w_branch, w_out, *consts)


def _rope_tables(n_ctx, n_lat):
    pos = jnp.arange(n_lat)
    row = (pos // GRID_W).astype(F32)
    col = (pos % GRID_W).astype(F32)

    def lane_pattern(rot_dim):
        half = rot_dim // 2
        inv_freq = ROPE_THETA ** (-jnp.arange(0, half, 2, dtype=F32) / half)
        ar, ac = row[:, None] * inv_freq[None, :], col[:, None] * inv_freq[None, :]
        cos = jnp.concatenate([jnp.cos(ar), jnp.cos(ar), jnp.cos(ac), jnp.cos(ac)], axis=1)
        sin = jnp.concatenate([-jnp.sin(ar), jnp.sin(ar), -jnp.sin(ac), jnp.sin(ac)], axis=1)
        cos = jnp.concatenate([jnp.ones((n_ctx, rot_dim), F32), cos], axis=0)
        sin = jnp.concatenate([jnp.zeros((n_ctx, rot_dim), F32), sin], axis=0)
        return cos, sin

    c64, s64 = lane_pattern(HEAD_DIM)
    c32, s32 = lane_pattern(DIFF_QK)
    t_all = n_ctx + n_lat
    rope_t = jnp.stack([jnp.tile(c64, (1, 2)), jnp.tile(s64, (1, 2)), jnp.tile(c32, (1, 4)), jnp.tile(s32, (1, 4))])
    ones, zeros = jnp.ones((MLA_NOPE, t_all), F32), jnp.zeros((MLA_NOPE, t_all), F32)
    pad1, pad0 = jnp.ones((32, t_all), F32), jnp.zeros((32, t_all), F32)
    rope_f = jnp.concatenate([c64.T, s64.T, c32.T, s32.T,
                              ones, c32.T, pad1, zeros, s32.T, pad0], axis=0)
    return rope_t, rope_f


def _layer_weights(l, w_in, gqa_q_norm, gqa_k_norm, mla_q_norm, mla_kv_norm, mla_w_uq, mla_w_ukv):
    w = w_in[l]
    col = lambda a, n: w[:, a:a + n]
    kpe = jnp.zeros((D_MODEL, 128), F32).at[:, 64:96].set(col(M0 + 384, 32))
    w_tok = jnp.concatenate([col(A0 + 256, 128), col(M0 + 256, 128), kpe, col(X0 + 256, 256),
                             col(S0 + 256, 128), col(G0, GATE_COLS)], axis=1)
    w_feat = jnp.concatenate([col(A0, 256), col(A0 + 384, 128), col(M0, 256), col(M0 + 256, 128),
                              col(X0, 256), col(X0 + 512, 256), col(S0, 256), col(S0 + 384, 128)], axis=1)
    uq = mla_w_uq[l].reshape(MLA_Q_RANK, MLA_HEADS, MLA_NOPE + MLA_ROPE)
    uq = jnp.pad(uq, ((0, 0), (0, 0), (0, 128 - MLA_NOPE - MLA_ROPE))).reshape(MLA_Q_RANK, 512)
    ukv = mla_w_ukv[l].reshape(MLA_KV_RANK, MLA_HEADS, 2 * MLA_NOPE)
    k_exp = jnp.pad(ukv[:, :, :MLA_NOPE], ((0, 0), (0, 0), (0, 64))).reshape(MLA_KV_RANK, 512)
    v_t = ukv[:, :, MLA_NOPE:].reshape(MLA_KV_RANK, 256).T
    return {
        "w_tok": w_tok.astype(MXU_DTYPE), "w_feat_t": w_feat.T.astype(MXU_DTYPE),
        "w_uq_t": uq.T.astype(MXU_DTYPE), "w_k_exp": k_exp.astype(MXU_DTYPE), "w_v_t": v_t.astype(MXU_DTYPE),
        "gk_row": jnp.tile(gqa_k_norm[l], 2)[None, :], "gkv_row": mla_kv_norm[l][None, :],
        "gq_col": gqa_q_norm[l][:, None], "gmq_col": mla_q_norm[l][:, None], "gkv_col": mla_kv_norm[l][:, None],
    }


def kernel(x, c, ctx, c_ctx, w_mod, b_mod, g_pre, g_post, w_ffn_gate, w_ffn_up, w_ffn_down, w_in,
           gqa_q_norm, gqa_k_norm, mla_q_norm, mla_kv_norm, mla_w_uq, mla_w_ukv,
           diff_lambda, diff_subln, swa_sink, w_branch, w_out):
    bsz, n_lat, d = x.shape
    n_ctx = ctx.shape[1]
    assert d == D_MODEL and n_ctx == TM and n_lat % (2 * TK) == 0 and bsz + 1 <= MOD_ROWS and bsz % SPS == 0
    n_ctx_tiles = n_ctx // TM

    cvec = jnp.zeros((MOD_ROWS, d), F32).at[:bsz].set(c).at[bsz].set(c_ctx)
    mod = _modulation(cvec, w_mod, b_mod).reshape(DEPTH, MOD_ROWS, 9, d)
    modsel = jnp.stack([jnp.broadcast_to(mod[:, bsz:bsz + 1], (DEPTH, bsz, 9, d)), mod[:, :bsz]], axis=2)

    rope_t, rope_f = _rope_tables(n_ctx, n_lat)
    xs = (ctx, x)
    cast = lambda a: a.astype(MXU_DTYPE)

    for l in range(DEPTH):
        ffn_w = lambda i: (cast(w_ffn_gate[l, i]), cast(w_ffn_up[l, i]), cast(w_ffn_down[l, i]))
        xs = _ffn_half(xs, modsel[l], g_pre[l, 0][None], g_post[l, 0][None], *ffn_w(0),
                       sub=0, n_ctx_tiles=n_ctx_tiles)
        w = _layer_weights(l, w_in, gqa_q_norm, gqa_k_norm, mla_q_norm, mla_kv_norm, mla_w_uq, mla_w_ukv)
        (k_a, k_m, k_d, k_s, gates, q_a, vt_a, q_m, vt_m, q_d, vt_d, q_s, vt_s) = _inproj(
            xs, modsel[l], g_pre[l, 1][None], w, rope_t, rope_f, n_ctx_tiles=n_ctx_tiles)
        lam_init = 0.8 - 0.6 * math.exp(-0.3 * l)
        lam_tab = jnp.zeros((8, 128), F32).at[0:4, 0:DIFF_QK].set(diff_lambda[l]).at[4, :].set(lam_init)
        sink_row = jnp.repeat(swa_sink[l], TM)[None, :]
        outs = (_dense_attention(q_a, k_a, vt_a, kind="gqa", n_ctx=n_ctx),
                _dense_attention(q_m, k_m, vt_m, kind="mla", n_ctx=n_ctx),
                _dense_attention(q_d, k_d, vt_d, kind="diff", n_ctx=n_ctx,
                                 lam_tab=lam_tab, subln_col=diff_subln[l][:, None]),
                _window_attention(q_s, k_s, vt_s, sink_row, n_ctx=n_ctx))
        xs = _merge_ffn(xs, modsel[l], g_post[l, 1][None], outs, gates, cast(w_branch[l]), cast(w_out[l]),
                        g_pre[l, 2][None], g_post[l, 2][None], *ffn_w(1), n_ctx_tiles=n_ctx_tiles,
                        latent_out=(l == DEPTH - 1))
    return xs
```

```python
import functools
import math

import jax
import jax.numpy as jnp
from jax import lax
from jax.experimental import pallas as pl
from jax.experimental.pallas import tpu as pltpu

MXU_DTYPE = jnp.bfloat16
F32 = jnp.float32

D_MODEL = 1024
D_FF = 2816
DEPTH = 2
GRID_W = 64
WINDOW = 128
ROPE_THETA = 10000.0
EPS = 1e-6
NEG_INF = -1e30
HEAD_DIM = 64
MLA_HEADS = 4
MLA_Q_RANK = 256
MLA_KV_RANK = 128
MLA_NOPE = 64
MLA_ROPE = 32
DIFF_QK = 32
N_BRANCH = 4
BRANCH_W = 256
GQA_SCALE = HEAD_DIM ** -0.5
MLA_SCALE = (MLA_NOPE + MLA_ROPE) ** -0.5
DIFF_SCALE = DIFF_QK ** -0.5
LOG2E = math.log2(math.e)
ONES_ROWS = 16
ACC_ROWS = HEAD_DIM + ONES_ROWS

A0, M0, X0, S0, G0 = 0, 512, 928, 1696, 2208
GATE_COLS = N_BRANCH * D_MODEL
TOK_COLS = 768
FEAT_ROWS = 1664

TM = 256
SPS = 2
TK = 512
STAGES_PER_TRIP = 4
MOD_ROWS = 16
MOD_TN = 1024
V7X_VMEM_LIMIT = 56 * 1024 * 1024


def _cparams(n_axes):
    return pltpu.CompilerParams(dimension_semantics=("arbitrary",) * n_axes,
                                vmem_limit_bytes=V7X_VMEM_LIMIT)


def _const_spec(shape):
    n = len(shape)
    return pl.BlockSpec(shape, lambda *_: (0,) * n, pipeline_mode=pl.Buffered(1))


def _per_sample(body, batched):
    def kernel(*refs):
        for i in range(SPS):
            body(*[r.at[i:i + 1] if b else r for r, b in zip(refs, batched)])
    return kernel


def _sigmoid(v):
    return 1.0 / (1.0 + jnp.exp(-v))


def _dot(a, b):
    return jnp.dot(a, b, preferred_element_type=F32)


def _rms_scale(v, axis):
    return lax.rsqrt(jnp.mean(v * v, axis=axis, keepdims=True) + EPS)


def _mod_kernel(c_ref, w_ref, b_ref, o_ref):
    c = c_ref[...]
    s = (c * _sigmoid(c)).astype(MXU_DTYPE)
    o_ref[0] = _dot(s, w_ref[0].astype(MXU_DTYPE)) + b_ref[0]


def _modulation(cvec, w_mod, b_mod):
    depth, d, n = w_mod.shape
    return pl.pallas_call(
        _mod_kernel,
        grid=(depth, n // MOD_TN),
        in_specs=[pl.BlockSpec((MOD_ROWS, d), lambda l, j: (0, 0)),
                  pl.BlockSpec((1, d, MOD_TN), lambda l, j: (l, 0, j)),
                  pl.BlockSpec((1, 1, MOD_TN), lambda l, j: (l, 0, j))],
        out_specs=pl.BlockSpec((1, MOD_ROWS, MOD_TN), lambda l, j: (l, 0, j)),
        out_shape=jax.ShapeDtypeStruct((depth, MOD_ROWS, n), F32),
        compiler_params=_cparams(2),
        name="modulation",
    )(cvec, w_mod, b_mod.reshape(depth, 1, n))


def _adaln(x, g_row, shift, scale):
    return (x * _rms_scale(x, -1) * g_row) * (1.0 + scale) + shift


def _ffn_residual(x, mod_ref, gpre_ref, gpost_ref, wg_ref, wu_ref, wd_ref, sub):
    shift = mod_ref[0, 0, 3 * sub:3 * sub + 1, :]
    scale = mod_ref[0, 0, 3 * sub + 1:3 * sub + 2, :]
    gate = mod_ref[0, 0, 3 * sub + 2:3 * sub + 3, :]
    u = _adaln(x, gpre_ref[...], shift, scale).astype(MXU_DTYPE)
    a = _dot(u, wg_ref[...])
    b = _dot(u, wu_ref[...])
    h = ((a * _sigmoid(a)) * b).astype(MXU_DTYPE)
    y = _dot(h, wd_ref[...])
    yn = y * _rms_scale(y, -1) * gpost_ref[...]
    return x + 0.5 * gate * yn


def _ffn_kernel(*refs, sub, n_ctx_tiles, split_in):
    if split_in:
        ctx_ref, lat_ref, mod_ref, gpre_ref, gpost_ref, wg_ref, wu_ref, wd_ref, o_ref = refs
        x = jnp.where(pl.program_id(1) < n_ctx_tiles, ctx_ref[0], lat_ref[0])
    else:
        x_ref, mod_ref, gpre_ref, gpost_ref, wg_ref, wu_ref, wd_ref, o_ref = refs
        x = x_ref[0]
    o_ref[0] = _ffn_residual(x, mod_ref, gpre_ref, gpost_ref, wg_ref, wu_ref, wd_ref, sub)


def _token_specs(n_ctx_tiles, first=0):
    x_spec = pl.BlockSpec((SPS, TM, D_MODEL), lambda b, t: (b, t + first, 0))
    mod_spec = pl.BlockSpec((SPS, 1, 9, D_MODEL),
                            lambda b, t: (b, jnp.where(t + first < n_ctx_tiles, 0, 1), 0, 0))
    return x_spec, mod_spec


def _latent_spec(n_ctx_tiles):
    return pl.BlockSpec((SPS, TM, D_MODEL), lambda b, t: (b, jnp.maximum(t - n_ctx_tiles, 0), 0))


def _ffn_half(xs, modsel, g_pre, g_post, wg, wu, wd, *, sub, n_ctx_tiles):
    split_in = isinstance(xs, tuple)
    x_spec, mod_spec = _token_specs(n_ctx_tiles)
    if split_in:
        ctx, lat = xs
        bsz, t_all, d = lat.shape[0], ctx.shape[1] + lat.shape[1], lat.shape[2]
        x_specs = [pl.BlockSpec((SPS, TM, d), lambda b, t: (b, jnp.minimum(t, n_ctx_tiles - 1), 0)),
                   _latent_spec(n_ctx_tiles)]
        x_args = [ctx, lat]
    else:
        bsz, t_all, d = xs.shape
        x_specs, x_args = [x_spec], [xs]
    batched = (True,) * (len(x_args) + 1) + (False,) * 5 + (True,)
    return pl.pallas_call(
        _per_sample(functools.partial(_ffn_kernel, sub=sub, n_ctx_tiles=n_ctx_tiles, split_in=split_in), batched),
        grid=(bsz // SPS, t_all // TM),
        in_specs=x_specs + [mod_spec, _const_spec((1, d)), _const_spec((1, d)),
                            _const_spec(wg.shape), _const_spec(wu.shape), _const_spec(wd.shape)],
        out_specs=x_spec,
        out_shape=jax.ShapeDtypeStruct((bsz, t_all, d), F32),
        compiler_params=_cparams(2),
        name=f"ffn_half_{sub}",
    )(*x_args, modsel, g_pre, g_post, wg, wu, wd)


def _rope_lanes(v, cos, sin, pair):
    width = v.shape[1]
    lane = lax.broadcasted_iota(jnp.int32, v.shape, 1)
    first = (lane & pair) == 0
    partner = jnp.where(first, pltpu.roll(v, width - pair, 1), pltpu.roll(v, pair, 1))
    return v * cos + partner * sin


def _swap_row_blocks(v, pair):
    parts = []
    for r in range(0, v.shape[0], 2 * pair):
        parts.append(v[r + pair:r + 2 * pair])
        parts.append(v[r:r + pair])
    return jnp.concatenate(parts, axis=0)


def _rope_rows(v, cos, sin, pair):
    return v * cos + _swap_row_blocks(v, pair) * sin


def _inproj_kernel(x_ref, mod_ref, gpre_ref, wtok_ref, wft_ref, wuq_ref, wkx_ref, wv_ref,
                   gk_ref, gkv_row_ref, gq_col_ref, gmq_col_ref, gkv_col_ref,
                   rope_t_ref, rope_f_ref,
                   ka_ref, km_ref, kd_ref, ks_ref, g_ref,
                   qa_ref, vta_ref, qm_ref, vtm_ref, qd_ref, vtd_ref, qs_ref, vts_ref):
    x = x_ref[0]
    shift = mod_ref[0, 0, 3:4, :]
    scale = mod_ref[0, 0, 4:5, :]
    u = _adaln(x, gpre_ref[...], shift, scale).astype(MXU_DTYPE)

    c64, s64, c32, s32 = rope_t_ref[0], rope_t_ref[1], rope_t_ref[2], rope_t_ref[3]

    ht = _dot(u, wtok_ref[:, 0:TOK_COLS])
    k_a = ht[:, 0:128]
    sq = k_a * k_a
    lane = lax.broadcasted_iota(jnp.int32, sq.shape, 1)
    low = lane < HEAD_DIM
    ss_lo = jnp.sum(jnp.where(low, sq, 0.0), axis=1, keepdims=True)
    ss_hi = jnp.sum(jnp.where(low, 0.0, sq), axis=1, keepdims=True)
    ms = jnp.where(low, ss_lo, ss_hi) * (1.0 / HEAD_DIM)
    k_a = k_a * lax.rsqrt(ms + EPS) * gk_ref[...]
    ka_ref[0] = _rope_lanes(k_a, c64, s64, 16).astype(MXU_DTYPE)

    ckv = ht[:, 128:256]
    ckv_n = (ckv * _rms_scale(ckv, -1) * gkv_row_ref[...]).astype(MXU_DTYPE)
    k_nope = _dot(ckv_n, wkx_ref[...])
    k_pe = _rope_lanes(ht[:, 256:384], c32, s32, 8)
    for h in range(MLA_HEADS):
        km_ref[0, :, 128 * h:128 * h + 128] = (k_nope[:, 128 * h:128 * h + 128] + k_pe).astype(MXU_DTYPE)

    for j in range(2):
        kd_ref[0, :, 128 * j:128 * j + 128] = _rope_lanes(
            ht[:, 384 + 128 * j:512 + 128 * j], c32, s32, 8).astype(MXU_DTYPE)
    ks_ref[0] = _rope_lanes(ht[:, 640:768], c64, s64, 16).astype(MXU_DTYPE)

    for j in range(GATE_COLS // D_MODEL):
        lo = TOK_COLS + j * D_MODEL
        hg = _dot(u, wtok_ref[:, lo:lo + D_MODEL])
        g_ref[0, :, j * D_MODEL:(j + 1) * D_MODEL] = _sigmoid(hg).astype(g_ref.dtype)

    hf = lax.dot_general(wft_ref[...], u, (((1,), (1,)), ((), ())), preferred_element_type=F32)
    fc64, fs64 = rope_f_ref[0:64, :], rope_f_ref[64:128, :]
    fc32, fs32 = rope_f_ref[128:160, :], rope_f_ref[160:192, :]
    fcm, fsm = rope_f_ref[192:320, :], rope_f_ref[320:448, :]
    zeros64 = jnp.zeros((HEAD_DIM, TM), MXU_DTYPE)

    def place_heads(ref, heads):
        for h, q in enumerate(heads):
            r0 = HEAD_DIM * (h // 2)
            ref[0, 0, r0:r0 + HEAD_DIM, h * TM:(h + 1) * TM] = q
            z0 = HEAD_DIM - r0
            ref[0, 0, z0:z0 + HEAD_DIM, h * TM:(h + 1) * TM] = zeros64

    q_heads = []
    for h in range(4):
        q = hf[64 * h:64 * h + 64]
        q = q * _rms_scale(q, 0) * gq_col_ref[...]
        q_heads.append((_rope_rows(q, fc64, fs64, 16) * (GQA_SCALE * LOG2E)).astype(MXU_DTYPE))
    place_heads(qa_ref, q_heads)
    vta_ref[0] = hf[256:384].astype(MXU_DTYPE)

    cq = hf[384:640]
    cq_n = (cq * _rms_scale(cq, 0) * gmq_col_ref[...]).astype(MXU_DTYPE)
    qm = _dot(wuq_ref[...], cq_n)
    for h in range(MLA_HEADS):
        qh = qm[128 * h:128 * h + 128]
        qm_ref[0, 128 * h:128 * h + 128, :] = (_rope_rows(qh, fcm, fsm, 8) * (MLA_SCALE * LOG2E)).astype(MXU_DTYPE)
    ckv_f = hf[640:768]
    ckv_fn = (ckv_f * _rms_scale(ckv_f, 0) * gkv_col_ref[...]).astype(MXU_DTYPE)
    vtm_ref[0] = _dot(wv_ref[...], ckv_fn).astype(MXU_DTYPE)

    zeros32 = jnp.zeros((DIFF_QK, TM), MXU_DTYPE)
    for c in range(2):
        for rb in range(4):
            r0 = 768 + 128 * c + 32 * rb
            piece = (_rope_rows(hf[r0:r0 + 32], fc32, fs32, 8) * (DIFF_SCALE * LOG2E)).astype(MXU_DTYPE)
            for cb in range(4):
                qd_ref[0, 0, c, 32 * rb:32 * rb + 32, cb * TM:(cb + 1) * TM] = piece if cb == rb else zeros32
    vtd_ref[0] = hf[1024:1280].astype(MXU_DTYPE)

    s_heads = []
    for h in range(4):
        q = hf[1280 + 64 * h:1344 + 64 * h]
        s_heads.append((_rope_rows(q, fc64, fs64, 16) * (GQA_SCALE * LOG2E)).astype(MXU_DTYPE))
    place_heads(qs_ref, s_heads)
    vts_ref[0] = hf[1536:1664].astype(MXU_DTYPE)


def _inproj(xs, modsel, g_pre, w, rope_t, rope_f, *, n_ctx_tiles):
    bsz, t_all, d = xs.shape
    nt = t_all // TM
    bt = lambda t, b: (b, t, 0)
    x_spec = pl.BlockSpec((SPS, TM, d), bt)
    mod_spec = pl.BlockSpec((SPS, 1, 9, d), lambda t, b: (b, jnp.where(t < n_ctx_tiles, 0, 1), 0, 0))
    tok = lambda n: (pl.BlockSpec((SPS, TM, n), bt), jax.ShapeDtypeStruct((bsz, t_all, n), MXU_DTYPE))
    feat = lambda n: (pl.BlockSpec((SPS, n, TM), lambda t, b: (b, 0, t)),
                      jax.ShapeDtypeStruct((bsz, n, t_all), MXU_DTYPE))
    qblk = (pl.BlockSpec((SPS, 1, 128, 4 * TM), lambda t, b: (b, t, 0, 0)),
            jax.ShapeDtypeStruct((bsz, nt, 128, 4 * TM), MXU_DTYPE))
    qdblk = (pl.BlockSpec((SPS, 1, 2, 128, 4 * TM), lambda t, b: (b, t, 0, 0, 0)),
             jax.ShapeDtypeStruct((bsz, nt, 2, 128, 4 * TM), MXU_DTYPE))
    outs = [tok(128), tok(512), tok(256), tok(128), tok(GATE_COLS),
            qblk, feat(128), feat(512), feat(256), qdblk, feat(256), qblk, feat(128)]
    consts = [g_pre, w["w_tok"], w["w_feat_t"], w["w_uq_t"], w["w_k_exp"], w["w_v_t"],
              w["gk_row"], w["gkv_row"], w["gq_col"], w["gmq_col"], w["gkv_col"]]
    batched = (True, True) + (False,) * (len(consts) + 2) + (True,) * len(outs)
    return pl.pallas_call(
        _per_sample(_inproj_kernel, batched),
        grid=(nt, bsz // SPS),
        in_specs=[x_spec, mod_spec] + [_const_spec(c.shape) for c in consts]
        + [pl.BlockSpec((4, TM, 128), lambda t, b: (0, t, 0)),
           pl.BlockSpec((448, TM), lambda t, b: (0, t))],
        out_specs=[o[0] for o in outs],
        out_shape=[o[1] for o in outs],
        compiler_params=_cparams(2),
        name="mixer_inproj",
    )(xs, modsel, *consts, rope_t, rope_f)


def _value_lhs(vt, r0):
    ones = jnp.ones((ONES_ROWS, vt.shape[1]), MXU_DTYPE)
    return jnp.concatenate([vt[r0:r0 + HEAD_DIM, :], ones], axis=0)


def _dense_attn_kernel(*refs, units, tk, n_ctx, n_lat, diff):
    if diff:
        (q1_ref, q2_ref, k_ref, vt_ref, lam_ref, subln_ref, o_ref,
         sa_ref, sb_ref, cma_ref, cmb_ref, m_ref, acc_ref) = refs
    else:
        q1_ref, q2_ref, k_ref, vt_ref, o_ref, sa_ref, sb_ref, cma_ref, cmb_ref, m_ref, acc_ref = refs
    step = pl.program_id(1)
    n_tiles = n_lat // tk
    nu = len(units)
    q_refs = (q1_ref, q2_ref)
    both, second = (0, 1), (1,)
    m_ref[...] = jnp.full(m_ref.shape, NEG_INF, F32)
    acc_ref[...] = jnp.zeros(acc_ref.shape, F32)
    buf_a, buf_b = (sa_ref, cma_ref), (sb_ref, cmb_ref)

    col_blocks = [(qi, u, c0, r0) for qi in both for u, (_, _, blocks) in enumerate(units) for c0, r0 in blocks]

    def scores_block(buf, off, size, qi, u, c0):
        s_ref, cm_ref = buf
        q_idx, k0, _ = units[u]
        s = _dot(k_ref[0, pl.ds(off, size), k0:k0 + 128], q_refs[qi][q_idx][:, c0:c0 + TM])
        s_ref[qi * nu + u, 0:size, c0:c0 + TM] = s
        cm_ref[qi * nu + u, :, c0:c0 + TM] = jnp.max(s, axis=0, keepdims=True)

    def consume_block(buf, off, size, qi, u, c0, r0):
        s_ref, cm_ref = buf
        w = qi * nu + u
        m_old = m_ref[w, :, c0:c0 + TM]
        m_new = jnp.maximum(m_old, cm_ref[w, :, c0:c0 + TM])
        alpha = jnp.exp2(m_old - m_new)
        m_ref[w, :, c0:c0 + TM] = m_new
        p = jnp.exp2(s_ref[w, 0:size, c0:c0 + TM] - m_new).astype(MXU_DTYPE)
        pv = _dot(_value_lhs(vt_ref[0, :, pl.ds(off, size)], r0), p)
        acc_ref[w, :, c0:c0 + TM] = alpha * acc_ref[w, :, c0:c0 + TM] + pv

    def scores(buf, off, size, halves):
        for qi, u, c0, _ in col_blocks:
            if qi in halves:
                scores_block(buf, off, size, qi, u, c0)

    def consume(buf, off, size, halves):
        for qi, u, c0, r0 in col_blocks:
            if qi in halves:
                consume_block(buf, off, size, qi, u, c0, r0)

    def finalize(halves):
        if diff:
            lf = lam_ref[...]
            lam_init = lam_ref[4:5, 0:1]
            lam = (jnp.exp(jnp.sum(lf[0:1, 0:DIFF_QK] * lf[1:2, 0:DIFF_QK], axis=1, keepdims=True))
                   - jnp.exp(jnp.sum(lf[2:3, 0:DIFF_QK] * lf[3:4, 0:DIFF_QK], axis=1, keepdims=True)) + lam_init)
        for qi in halves:
            out0 = qi * TM
            for u, (_, _, blocks) in enumerate(units):
                w = qi * nu + u
                inv = 1.0 / acc_ref[w, HEAD_DIM:HEAD_DIM + 1, :]
                if diff:
                    for hh in range(2):
                        c1, c2 = 2 * hh * TM, (2 * hh + 1) * TM
                        o = (acc_ref[w, 0:HEAD_DIM, c1:c1 + TM] * inv[:, c1:c1 + TM]
                             - lam * (acc_ref[w, 0:HEAD_DIM, c2:c2 + TM] * inv[:, c2:c2 + TM]))
                        o = o * _rms_scale(o, 0) * subln_ref[...] * (1.0 - lam_init)
                        h = 2 * u + hh
                        o_ref[0, HEAD_DIM * h:HEAD_DIM * h + HEAD_DIM, out0:out0 + TM] = o
                else:
                    for c0, _ in blocks:
                        h = (u * len(blocks) * TM + c0) // TM
                        o_ref[0, HEAD_DIM * h:HEAD_DIM * h + HEAD_DIM, out0:out0 + TM] = (
                            acc_ref[w, 0:HEAD_DIM, c0:c0 + TM] * inv[:, c0:c0 + TM])

    def lat(j):
        return n_ctx + j * tk if isinstance(j, int) else pl.multiple_of(n_ctx + j * tk, 128)

    @pl.when(step == 0)
    def _():
        scores(buf_a, 0, n_ctx, second)
        consume(buf_a, 0, n_ctx, second)
        o_ref[0, :, 0:TM] = jnp.zeros((BRANCH_W, TM), F32)
        finalize(second)

    @pl.when(step != 0)
    def _():
        buf_of = lambda parity: buf_a if parity % 2 else buf_b

        def stage(parity, j):
            for qi, u, c0, r0 in col_blocks:
                scores_block(buf_of(parity + 1), lat(j + 1), tk, qi, u, c0)
                consume_block(buf_of(parity), lat(j), tk, qi, u, c0, r0)

        scores(buf_a, 0, n_ctx, both)
        for qi, u, c0, r0 in col_blocks:
            scores_block(buf_b, n_ctx, tk, qi, u, c0)
            consume_block(buf_a, 0, n_ctx, qi, u, c0, r0)
        n_loop = (n_tiles - 1) // STAGES_PER_TRIP

        def body(i, carry):
            for r in range(STAGES_PER_TRIP):
                stage(r, STAGES_PER_TRIP * i + r)
            return carry

        lax.fori_loop(0, n_loop, body, 0)
        for j in range(STAGES_PER_TRIP * n_loop, n_tiles - 1):
            stage(j, j)
        consume(buf_of(n_tiles - 1), lat(n_tiles - 1), tk, both)
        finalize(both)


def _dense_attention(q, k, vt, *, kind, n_ctx, lam_tab=None, subln_col=None):
    bsz, t_all, dk = k.shape
    dv = vt.shape[1]
    n_lat = t_all - n_ctx
    n_steps = 1 + (n_lat // TM) // 2
    tiles = (lambda t: jnp.maximum(2 * t - 1, 0), lambda t: 2 * t)
    if kind == "gqa":
        q_specs = [pl.BlockSpec((1, 1, 128, 4 * TM), lambda b, t, f=f: (b, f(t), 0, 0)) for f in tiles]
        units = (((0, 0), 0, tuple((h * TM, HEAD_DIM * (h // 2)) for h in range(4))),)
        ncols = 4 * TM
    elif kind == "diff":
        q_specs = [pl.BlockSpec((1, 1, 2, 128, 4 * TM), lambda b, t, f=f: (b, f(t), 0, 0, 0)) for f in tiles]
        units = tuple(((0, 0, u), 128 * u,
                       tuple(((2 * hh + mm) * TM, HEAD_DIM * (2 * u + hh)) for hh in range(2) for mm in range(2)))
                      for u in range(2))
        ncols = 4 * TM
    else:
        q_specs = [pl.BlockSpec((1, 512, TM), lambda b, t, f=f: (b, 0, f(t))) for f in tiles]
        units = tuple(((0, slice(128 * h, 128 * h + 128)), 128 * h, ((0, HEAD_DIM * h),)) for h in range(4))
        ncols = TM
    in_specs = q_specs + [pl.BlockSpec((1, t_all, dk), lambda b, t: (b, 0, 0)),
                          pl.BlockSpec((1, dv, t_all), lambda b, t: (b, 0, 0))]
    args = [q, q, k, vt]
    if kind == "diff":
        in_specs += [_const_spec(lam_tab.shape), _const_spec(subln_col.shape)]
        args += [lam_tab, subln_col]
    nw = 2 * len(units)
    return pl.pallas_call(
        functools.partial(_dense_attn_kernel, units=units, tk=TK, n_ctx=n_ctx, n_lat=n_lat,
                          diff=(kind == "diff")),
        grid=(bsz, n_steps),
        in_specs=in_specs,
        out_specs=pl.BlockSpec((1, BRANCH_W, 2 * TM), lambda b, t: (b, 0, t)),
        out_shape=jax.ShapeDtypeStruct((bsz, BRANCH_W, 2 * TM * n_steps), F32),
        scratch_shapes=[pltpu.VMEM((nw, TK, ncols), F32), pltpu.VMEM((nw, TK, ncols), F32),
                        pltpu.VMEM((nw, 1, ncols), F32), pltpu.VMEM((nw, 1, ncols), F32),
                        pltpu.VMEM((nw, 1, ncols), F32), pltpu.VMEM((nw, ACC_ROWS, ncols), F32)],
        compiler_params=_cparams(2),
        name=f"attn_{kind}",
    )(*args)


def _window_attn_kernel(q1_ref, q2_ref, k_ref, vt_ref, sink_ref, o_ref,
                        sc_ref, sw_ref, bias_ref, cmc_ref, cmw_ref, m_ref, acc_ref, *, n_ctx, t_all, wk):
    step = pl.program_id(1)
    q_refs = (q1_ref, q2_ref)
    both, second = (0, 1), (1,)
    m_ref[...] = jnp.broadcast_to(sink_ref[...] * LOG2E, m_ref.shape)
    acc_ref[:, 0:HEAD_DIM, :] = jnp.zeros((2, HEAD_DIM, 4 * TM), F32)
    acc_ref[:, HEAD_DIM:ACC_ROWS, :] = jnp.ones((2, ONES_ROWS, 4 * TM), F32)

    def window(h):
        q0 = (2 * step - 1 + h) * TM
        start = pl.multiple_of(jnp.clip(q0 - WINDOW, n_ctx, t_all - wk), 128)
        kpos = start + lax.broadcasted_iota(jnp.int32, (wk, TM), 0)
        qpos = q0 + lax.broadcasted_iota(jnp.int32, (wk, TM), 1)
        allowed = (jnp.abs(kpos - qpos) <= WINDOW) & (qpos >= n_ctx)
        bias_ref[h] = jnp.where(allowed, 0.0, NEG_INF)
        return start

    def scores_ctx(h, hd):
        c = slice(hd * TM, (hd + 1) * TM)
        s = _dot(k_ref[0, 0:n_ctx, :], q_refs[h][0, 0, :, c])
        sc_ref[h, :, c] = s
        cmc_ref[h, :, c] = jnp.max(s, axis=0, keepdims=True)

    def scores_win(h, hd, start):
        c = slice(hd * TM, (hd + 1) * TM)
        s = _dot(k_ref[0, pl.ds(start, wk), :], q_refs[h][0, 0, :, c]) + bias_ref[h]
        sw_ref[h, :, c] = s
        cmw_ref[h, :, c] = jnp.max(s, axis=0, keepdims=True)

    def consume(h, hd, s, col_max, vt):
        c = slice(hd * TM, (hd + 1) * TM)
        m_old = m_ref[h, :, c]
        m_new = jnp.maximum(m_old, col_max)
        alpha = jnp.exp2(m_old - m_new)
        m_ref[h, :, c] = m_new
        pv = _dot(_value_lhs(vt, HEAD_DIM * (hd // 2)), jnp.exp2(s - m_new).astype(MXU_DTYPE))
        acc_ref[h, :, c] = alpha * acc_ref[h, :, c] + pv

    def consume_ctx(h, hd):
        c = slice(hd * TM, (hd + 1) * TM)
        consume(h, hd, sc_ref[h, :, c], cmc_ref[h, :, c], vt_ref[0, :, 0:n_ctx])

    def finalize(halves):
        for h in halves:
            inv = 1.0 / acc_ref[h, HEAD_DIM:HEAD_DIM + 1, :]
            for hd in range(4):
                c = slice(hd * TM, (hd + 1) * TM)
                o_ref[0, HEAD_DIM * hd:HEAD_DIM * hd + HEAD_DIM, h * TM:(h + 1) * TM] = acc_ref[h, 0:HEAD_DIM, c] * inv[:, c]

    @pl.when(step == 0)
    def _():
        for hd in range(4):
            scores_ctx(1, hd)
        for hd in range(4):
            consume_ctx(1, hd)
        o_ref[0, :, 0:TM] = jnp.zeros((BRANCH_W, TM), F32)
        finalize(second)

    @pl.when(step != 0)
    def _():
        blocks = [(h, hd) for h in both for hd in range(4)]
        wins = [window(h) for h in both]
        for h, hd in blocks:
            scores_ctx(h, hd)
        for h, hd in blocks:
            scores_win(h, hd, wins[h])
            consume_ctx(h, hd)
        for h, hd in blocks:
            c = slice(hd * TM, (hd + 1) * TM)
            consume(h, hd, sw_ref[h, :, c], cmw_ref[h, :, c], vt_ref[0, :, pl.ds(wins[h], wk)])
        finalize(both)


def _window_attention(q, k, vt, sink_row, *, n_ctx):
    bsz, t_all, dk = k.shape
    wk = TM + 2 * WINDOW
    n_steps = 1 + ((t_all - n_ctx) // TM) // 2
    tiles = (lambda t: jnp.maximum(2 * t - 1, 0), lambda t: 2 * t)
    q_specs = [pl.BlockSpec((1, 1, 128, 4 * TM), lambda b, t, f=f: (b, f(t), 0, 0)) for f in tiles]
    return pl.pallas_call(
        functools.partial(_window_attn_kernel, n_ctx=n_ctx, t_all=t_all, wk=wk),
        grid=(bsz, n_steps),
        in_specs=q_specs + [pl.BlockSpec((1, t_all, dk), lambda b, t: (b, 0, 0)),
                            pl.BlockSpec((1, vt.shape[1], t_all), lambda b, t: (b, 0, 0)),
                            _const_spec(sink_row.shape)],
        out_specs=pl.BlockSpec((1, BRANCH_W, 2 * TM), lambda b, t: (b, 0, t)),
        out_shape=jax.ShapeDtypeStruct((bsz, BRANCH_W, 2 * TM * n_steps), F32),
        scratch_shapes=[pltpu.VMEM((2, n_ctx, 4 * TM), F32), pltpu.VMEM((2, wk, 4 * TM), F32),
                        pltpu.VMEM((2, wk, TM), F32),
                        pltpu.VMEM((2, 1, 4 * TM), F32), pltpu.VMEM((2, 1, 4 * TM), F32),
                        pltpu.VMEM((2, 1, 4 * TM), F32), pltpu.VMEM((2, ACC_ROWS, 4 * TM), F32)],
        compiler_params=_cparams(2),
        name="attn_window",
    )(q, q, k, vt, sink_row)


def _merge_ffn_kernel(x_ref, mod_ref, gpost1_ref, oa_ref, om_ref, od_ref, os_ref, g_ref, wb_ref, wo_ref,
                      gpre2_ref, gpost2_ref, wg_ref, wu_ref, wd_ref, o_ref):
    x = x_ref[0]
    gate = mod_ref[0, 0, 5:6, :]
    y = None
    for i, o_t in enumerate((oa_ref, om_ref, od_ref, os_ref)):
        o_tok = o_t[0].T.astype(MXU_DTYPE)
        term = g_ref[0, :, i * D_MODEL:(i + 1) * D_MODEL].astype(F32) * _dot(o_tok, wb_ref[i])
        y = term if y is None else y + term
    z = _dot(y.astype(MXU_DTYPE), wo_ref[...])
    x = x + gate * (z * _rms_scale(z, -1) * gpost1_ref[...])
    o_ref[0] = _ffn_residual(x, mod_ref, gpre2_ref, gpost2_ref, wg_ref, wu_ref, wd_ref, 2)


def _merge_ffn(xs, modsel, g_post1, outs, gates, w_branch, w_out, g_pre2, g_post2, wg, wu, wd,
               *, n_ctx_tiles, latent_out):
    bsz, t_all, d = xs.shape
    first = n_ctx_tiles if latent_out else 0
    x_spec, mod_spec = _token_specs(n_ctx_tiles, first)
    o_spec = pl.BlockSpec((SPS, BRANCH_W, TM), lambda b, t: (b, 0, t + first + 1))
    consts = [g_pre2, g_post2, wg, wu, wd]
    batched = (True, True, False) + (True,) * 5 + (False,) * (2 + len(consts)) + (True,)
    return pl.pallas_call(
        _per_sample(_merge_ffn_kernel, batched),
        grid=(bsz // SPS, t_all // TM - first),
        in_specs=[x_spec, mod_spec, _const_spec((1, d)), o_spec, o_spec, o_spec, o_spec,
                  pl.BlockSpec((SPS, TM, GATE_COLS), lambda b, t: (b, t + first, 0)),
                  _const_spec(w_branch.shape), _const_spec(w_out.shape)] + [_const_spec(c.shape) for c in consts],
        out_specs=pl.BlockSpec((SPS, TM, d), lambda b, t: (b, t, 0)),
        out_shape=jax.ShapeDtypeStruct((bsz, t_all - first * TM, d), F32),
        compiler_params=_cparams(2),
        name="merge_ffn_half_2",
    )(xs, modsel, g_post1, *outs, gates, w_branch, w_out, *consts)


def _rope_tables(n_ctx, n_lat):
    pos = jnp.arange(n_lat)
    row = (pos // GRID_W).astype(F32)
    col = (pos % GRID_W).astype(F32)

    def lane_pattern(rot_dim):
        half = rot_dim // 2
        inv_freq = ROPE_THETA ** (-jnp.arange(0, half, 2, dtype=F32) / half)
        ar, ac = row[:, None] * inv_freq[None, :], col[:, None] * inv_freq[None, :]
        cos = jnp.concatenate([jnp.cos(ar), jnp.cos(ar), jnp.cos(ac), jnp.cos(ac)], axis=1)
        sin = jnp.concatenate([-jnp.sin(ar), jnp.sin(ar), -jnp.sin(ac), jnp.sin(ac)], axis=1)
        cos = jnp.concatenate([jnp.ones((n_ctx, rot_dim), F32), cos], axis=0)
        sin = jnp.concatenate([jnp.zeros((n_ctx, rot_dim), F32), sin], axis=0)
        return cos, sin

    c64, s64 = lane_pattern(HEAD_DIM)
    c32, s32 = lane_pattern(DIFF_QK)
    t_all = n_ctx + n_lat
    rope_t = jnp.stack([jnp.tile(c64, (1, 2)), jnp.tile(s64, (1, 2)), jnp.tile(c32, (1, 4)), jnp.tile(s32, (1, 4))])
    ones, zeros = jnp.ones((MLA_NOPE, t_all), F32), jnp.zeros((MLA_NOPE, t_all), F32)
    pad1, pad0 = jnp.ones((32, t_all), F32), jnp.zeros((32, t_all), F32)
    rope_f = jnp.concatenate([c64.T, s64.T, c32.T, s32.T,
                              ones, c32.T, pad1, zeros, s32.T, pad0], axis=0)
    return rope_t, rope_f


def _layer_weights(l, w_in, gqa_q_norm, gqa_k_norm, mla_q_norm, mla_kv_norm, mla_w_uq, mla_w_ukv):
    w = w_in[l]
    col = lambda a, n: w[:, a:a + n]
    kpe = jnp.zeros((D_MODEL, 128), F32).at[:, 64:96].set(col(M0 + 384, 32))
    w_tok = jnp.concatenate([col(A0 + 256, 128), col(M0 + 256, 128), kpe, col(X0 + 256, 256),
                             col(S0 + 256, 128), col(G0, GATE_COLS)], axis=1)
    w_feat = jnp.concatenate([col(A0, 256), col(A0 + 384, 128), col(M0, 256), col(M0 + 256, 128),
                              col(X0, 256), col(X0 + 512, 256), col(S0, 256), col(S0 + 384, 128)], axis=1)
    uq = mla_w_uq[l].reshape(MLA_Q_RANK, MLA_HEADS, MLA_NOPE + MLA_ROPE)
    uq = jnp.pad(uq, ((0, 0), (0, 0), (0, 128 - MLA_NOPE - MLA_ROPE))).reshape(MLA_Q_RANK, 512)
    ukv = mla_w_ukv[l].reshape(MLA_KV_RANK, MLA_HEADS, 2 * MLA_NOPE)
    k_exp = jnp.pad(ukv[:, :, :MLA_NOPE], ((0, 0), (0, 0), (0, 64))).reshape(MLA_KV_RANK, 512)
    v_t = ukv[:, :, MLA_NOPE:].reshape(MLA_KV_RANK, 256).T
    return {
        "w_tok": w_tok.astype(MXU_DTYPE), "w_feat_t": w_feat.T.astype(MXU_DTYPE),
        "w_uq_t": uq.T.astype(MXU_DTYPE), "w_k_exp": k_exp.astype(MXU_DTYPE), "w_v_t": v_t.astype(MXU_DTYPE),
        "gk_row": jnp.tile(gqa_k_norm[l], 2)[None, :], "gkv_row": mla_kv_norm[l][None, :],
        "gq_col": gqa_q_norm[l][:, None], "gmq_col": mla_q_norm[l][:, None], "gkv_col": mla_kv_norm[l][:, None],
    }


def kernel(x, c, ctx, c_ctx, w_mod, b_mod, g_pre, g_post, w_ffn_gate, w_ffn_up, w_ffn_down, w_in,
           gqa_q_norm, gqa_k_norm, mla_q_norm, mla_kv_norm, mla_w_uq, mla_w_ukv,
           diff_lambda, diff_subln, swa_sink, w_branch, w_out):
    bsz, n_lat, d = x.shape
    n_ctx = ctx.shape[1]
    assert d == D_MODEL and n_ctx == TM and n_lat % (2 * TK) == 0 and bsz + 1 <= MOD_ROWS and bsz % SPS == 0
    n_ctx_tiles = n_ctx // TM

    cvec = jnp.zeros((MOD_ROWS, d), F32).at[:bsz].set(c).at[bsz].set(c_ctx)
    mod = _modulation(cvec, w_mod, b_mod).reshape(DEPTH, MOD_ROWS, 9, d)
    modsel = jnp.stack([jnp.broadcast_to(mod[:, bsz:bsz + 1], (DEPTH, bsz, 9, d)), mod[:, :bsz]], axis=2)

    rope_t, rope_f = _rope_tables(n_ctx, n_lat)
    xs = (ctx, x)
    cast = lambda a: a.astype(MXU_DTYPE)

    for l in range(DEPTH):
        ffn_w = lambda i: (cast(w_ffn_gate[l, i]), cast(w_ffn_up[l, i]), cast(w_ffn_down[l, i]))
        xs = _ffn_half(xs, modsel[l], g_pre[l, 0][None], g_post[l, 0][None], *ffn_w(0),
                       sub=0, n_ctx_tiles=n_ctx_tiles)
        w = _layer_weights(l, w_in, gqa_q_norm, gqa_k_norm, mla_q_norm, mla_kv_norm, mla_w_uq, mla_w_ukv)
        (k_a, k_m, k_d, k_s, gates, q_a, vt_a, q_m, vt_m, q_d, vt_d, q_s, vt_s) = _inproj(
            xs, modsel[l], g_pre[l, 1][None], w, rope_t, rope_f, n_ctx_tiles=n_ctx_tiles)
        lam_init = 0.8 - 0.6 * math.exp(-0.3 * l)
        lam_tab = jnp.zeros((8, 128), F32).at[0:4, 0:DIFF_QK].set(diff_lambda[l]).at[4, :].set(lam_init)
        sink_row = jnp.repeat(swa_sink[l], TM)[None, :]
        outs = (_dense_attention(q_a, k_a, vt_a, kind="gqa", n_ctx=n_ctx),
                _dense_attention(q_m, k_m, vt_m, kind="mla", n_ctx=n_ctx),
                _dense_attention(q_d, k_d, vt_d, kind="diff", n_ctx=n_ctx,
                                 lam_tab=lam_tab, subln_col=diff_subln[l][:, None]),
                _window_attention(q_s, k_s, vt_s, sink_row, n_ctx=n_ctx))
        xs = _merge_ffn(xs, modsel[l], g_post[l, 1][None], outs, gates, cast(w_branch[l]), cast(w_out[l]),
                        g_pre[l, 2][None], g_post[l, 2][None], *ffn_w(1), n_ctx_tiles=n_ctx_tiles,
                        latent_out=(l == DEPTH - 1))
    return xs
```

```python
import functools
import math

import jax
import jax.numpy as jnp
from jax import lax
from jax.experimental import pallas as pl
from jax.experimental.pallas import tpu as pltpu

MXU_DTYPE = jnp.bfloat16
F32 = jnp.float32

D_MODEL = 1024
D_FF = 2816
DEPTH = 2
GRID_W = 64
WINDOW = 128
ROPE_THETA = 10000.0
EPS = 1e-6
NEG_INF = -1e30
HEAD_DIM = 64
MLA_HEADS = 4
MLA_Q_RANK = 256
MLA_KV_RANK = 128
MLA_NOPE = 64
MLA_ROPE = 32
DIFF_QK = 32
N_BRANCH = 4
BRANCH_W = 256
GQA_SCALE = HEAD_DIM ** -0.5
MLA_SCALE = (MLA_NOPE + MLA_ROPE) ** -0.5
DIFF_SCALE = DIFF_QK ** -0.5
LOG2E = math.log2(math.e)
ONES_ROWS = 16
ACC_ROWS = HEAD_DIM + ONES_ROWS

A0, M0, X0, S0, G0 = 0, 512, 928, 1696, 2208
GATE_COLS = N_BRANCH * D_MODEL
TOK_COLS = 768
FEAT_ROWS = 1664

TM = 256
SPS = 2
TK = 512
STAGES_PER_TRIP = 4
MOD_ROWS = 16
MOD_TN = 1024
V7X_VMEM_LIMIT = 56 * 1024 * 1024


def _cparams(n_axes):
    return pltpu.CompilerParams(dimension_semantics=("arbitrary",) * n_axes,
                                vmem_limit_bytes=V7X_VMEM_LIMIT)


def _const_spec(shape):
    n = len(shape)
    return pl.BlockSpec(shape, lambda *_: (0,) * n, pipeline_mode=pl.Buffered(1))


def _per_sample(body, batched):
    def kernel(*refs):
        for i in range(SPS):
            body(*[r.at[i:i + 1] if b else r for r, b in zip(refs, batched)])
    return kernel


def _sigmoid(v):
    return 1.0 / (1.0 + jnp.exp(-v))


def _dot(a, b):
    return jnp.dot(a, b, preferred_element_type=F32)


def _rms_scale(v, axis):
    return lax.rsqrt(jnp.mean(v * v, axis=axis, keepdims=True) + EPS)


def _mod_kernel(c_ref, w_ref, b_ref, o_ref):
    c = c_ref[...]
    s = (c * _sigmoid(c)).astype(MXU_DTYPE)
    o_ref[0] = _dot(s, w_ref[0].astype(MXU_DTYPE)) + b_ref[0]


def _modulation(cvec, w_mod, b_mod):
    depth, d, n = w_mod.shape
    return pl.pallas_call(
        _mod_kernel,
        grid=(depth, n // MOD_TN),
        in_specs=[pl.BlockSpec((MOD_ROWS, d), lambda l, j: (0, 0)),
                  pl.BlockSpec((1, d, MOD_TN), lambda l, j: (l, 0, j)),
                  pl.BlockSpec((1, 1, MOD_TN), lambda l, j: (l, 0, j))],
        out_specs=pl.BlockSpec((1, MOD_ROWS, MOD_TN), lambda l, j: (l, 0, j)),
        out_shape=jax.ShapeDtypeStruct((depth, MOD_ROWS, n), F32),
        compiler_params=_cparams(2),
        name="modulation",
    )(cvec, w_mod, b_mod.reshape(depth, 1, n))


def _adaln(x, g_row, shift, scale):
    return (x * _rms_scale(x, -1) * g_row) * (1.0 + scale) + shift


def _ffn_residual(x, mod_ref, gpre_ref, gpost_ref, wg_ref, wu_ref, wd_ref, sub):
    shift = mod_ref[0, 0, 3 * sub:3 * sub + 1, :]
    scale = mod_ref[0, 0, 3 * sub + 1:3 * sub + 2, :]
    gate = mod_ref[0, 0, 3 * sub + 2:3 * sub + 3, :]
    u = _adaln(x, gpre_ref[...], shift, scale).astype(MXU_DTYPE)
    a = _dot(u, wg_ref[...])
    b = _dot(u, wu_ref[...])
    h = ((a * _sigmoid(a)) * b).astype(MXU_DTYPE)
    y = _dot(h, wd_ref[...])
    yn = y * _rms_scale(y, -1) * gpost_ref[...]
    return x + 0.5 * gate * yn


def _ffn_kernel(*refs, sub, n_ctx_tiles, split_in):
    if split_in:
        ctx_ref, lat_ref, mod_ref, gpre_ref, gpost_ref, wg_ref, wu_ref, wd_ref, o_ref = refs
        x = jnp.where(pl.program_id(1) < n_ctx_tiles, ctx_ref[0], lat_ref[0])
    else:
        x_ref, mod_ref, gpre_ref, gpost_ref, wg_ref, wu_ref, wd_ref, o_ref = refs
        x = x_ref[0]
    o_ref[0] = _ffn_residual(x, mod_ref, gpre_ref, gpost_ref, wg_ref, wu_ref, wd_ref, sub)


def _token_specs(n_ctx_tiles, first=0):
    x_spec = pl.BlockSpec((SPS, TM, D_MODEL), lambda b, t: (b, t + first, 0))
    mod_spec = pl.BlockSpec((SPS, 1, 9, D_MODEL),
                            lambda b, t: (b, jnp.where(t + first < n_ctx_tiles, 0, 1), 0, 0))
    return x_spec, mod_spec


def _latent_spec(n_ctx_tiles):
    return pl.BlockSpec((SPS, TM, D_MODEL), lambda b, t: (b, jnp.maximum(t - n_ctx_tiles, 0), 0))


def _ffn_half(xs, modsel, g_pre, g_post, wg, wu, wd, *, sub, n_ctx_tiles):
    split_in = isinstance(xs, tuple)
    x_spec, mod_spec = _token_specs(n_ctx_tiles)
    if split_in:
        ctx, lat = xs
        bsz, t_all, d = lat.shape[0], ctx.shape[1] + lat.shape[1], lat.shape[2]
        x_specs = [pl.BlockSpec((SPS, TM, d), lambda b, t: (b, jnp.minimum(t, n_ctx_tiles - 1), 0)),
                   _latent_spec(n_ctx_tiles)]
        x_args = [ctx, lat]
    else:
        bsz, t_all, d = xs.shape
        x_specs, x_args = [x_spec], [xs]
    batched = (True,) * (len(x_args) + 1) + (False,) * 5 + (True,)
    return pl.pallas_call(
        _per_sample(functools.partial(_ffn_kernel, sub=sub, n_ctx_tiles=n_ctx_tiles, split_in=split_in), batched),
        grid=(bsz // SPS, t_all // TM),
        in_specs=x_specs + [mod_spec, _const_spec((1, d)), _const_spec((1, d)),
                            _const_spec(wg.shape), _const_spec(wu.shape), _const_spec(wd.shape)],
        out_specs=x_spec,
        out_shape=jax.ShapeDtypeStruct((bsz, t_all, d), F32),
        compiler_params=_cparams(2),
        name=f"ffn_half_{sub}",
    )(*x_args, modsel, g_pre, g_post, wg, wu, wd)


def _rope_lanes(v, cos, sin, pair):
    width = v.shape[1]
    lane = lax.broadcasted_iota(jnp.int32, v.shape, 1)
    first = (lane & pair) == 0
    partner = jnp.where(first, pltpu.roll(v, width - pair, 1), pltpu.roll(v, pair, 1))
    return v * cos + partner * sin


def _swap_row_blocks(v, pair):
    parts = []
    for r in range(0, v.shape[0], 2 * pair):
        parts.append(v[r + pair:r + 2 * pair])
        parts.append(v[r:r + pair])
    return jnp.concatenate(parts, axis=0)


def _rope_rows(v, cos, sin, pair):
    return v * cos + _swap_row_blocks(v, pair) * sin


def _inproj_kernel(x_ref, mod_ref, gpre_ref, wtok_ref, wft_ref, wuq_ref, wkx_ref, wv_ref,
                   gk_ref, gkv_row_ref, gq_col_ref, gmq_col_ref, gkv_col_ref,
                   rope_t_ref, rope_f_ref,
                   ka_ref, km_ref, kd_ref, ks_ref, g_ref,
                   qa_ref, vta_ref, qm_ref, vtm_ref, qd_ref, vtd_ref, qs_ref, vts_ref):
    x = x_ref[0]
    shift = mod_ref[0, 0, 3:4, :]
    scale = mod_ref[0, 0, 4:5, :]
    u = _adaln(x, gpre_ref[...], shift, scale).astype(MXU_DTYPE)

    c64, s64, c32, s32 = rope_t_ref[0], rope_t_ref[1], rope_t_ref[2], rope_t_ref[3]

    ht = _dot(u, wtok_ref[:, 0:TOK_COLS])
    k_a = ht[:, 0:128]
    sq = k_a * k_a
    lane = lax.broadcasted_iota(jnp.int32, sq.shape, 1)
    low = lane < HEAD_DIM
    ss_lo = jnp.sum(jnp.where(low, sq, 0.0), axis=1, keepdims=True)
    ss_hi = jnp.sum(jnp.where(low, 0.0, sq), axis=1, keepdims=True)
    ms = jnp.where(low, ss_lo, ss_hi) * (1.0 / HEAD_DIM)
    k_a = k_a * lax.rsqrt(ms + EPS) * gk_ref[...]
    ka_ref[0] = _rope_lanes(k_a, c64, s64, 16).astype(MXU_DTYPE)

    ckv = ht[:, 128:256]
    ckv_n = (ckv * _rms_scale(ckv, -1) * gkv_row_ref[...]).astype(MXU_DTYPE)
    k_nope = _dot(ckv_n, wkx_ref[...])
    k_pe = _rope_lanes(ht[:, 256:384], c32, s32, 8)
    for h in range(MLA_HEADS):
        km_ref[0, :, 128 * h:128 * h + 128] = (k_nope[:, 128 * h:128 * h + 128] + k_pe).astype(MXU_DTYPE)

    for j in range(2):
        kd_ref[0, :, 128 * j:128 * j + 128] = _rope_lanes(
            ht[:, 384 + 128 * j:512 + 128 * j], c32, s32, 8).astype(MXU_DTYPE)
    ks_ref[0] = _rope_lanes(ht[:, 640:768], c64, s64, 16).astype(MXU_DTYPE)

    for j in range(GATE_COLS // D_MODEL):
        lo = TOK_COLS + j * D_MODEL
        hg = _dot(u, wtok_ref[:, lo:lo + D_MODEL])
        g_ref[0, :, j * D_MODEL:(j + 1) * D_MODEL] = _sigmoid(hg).astype(g_ref.dtype)

    hf = lax.dot_general(wft_ref[...], u, (((1,), (1,)), ((), ())), preferred_element_type=F32)
    fc64, fs64 = rope_f_ref[0:64, :], rope_f_ref[64:128, :]
    fc32, fs32 = rope_f_ref[128:160, :], rope_f_ref[160:192, :]
    fcm, fsm = rope_f_ref[192:320, :], rope_f_ref[320:448, :]
    zeros64 = jnp.zeros((HEAD_DIM, TM), MXU_DTYPE)

    def place_heads(ref, heads):
        for h, q in enumerate(heads):
            r0 = HEAD_DIM * (h // 2)
            ref[0, 0, r0:r0 + HEAD_DIM, h * TM:(h + 1) * TM] = q
            z0 = HEAD_DIM - r0
            ref[0, 0, z0:z0 + HEAD_DIM, h * TM:(h + 1) * TM] = zeros64

    q_heads = []
    for h in range(4):
        q = hf[64 * h:64 * h + 64]
        q = q * _rms_scale(q, 0) * gq_col_ref[...]
        q_heads.append((_rope_rows(q, fc64, fs64, 16) * (GQA_SCALE * LOG2E)).astype(MXU_DTYPE))
    place_heads(qa_ref, q_heads)
    vta_ref[0] = hf[256:384].astype(MXU_DTYPE)

    cq = hf[384:640]
    cq_n = (cq * _rms_scale(cq, 0) * gmq_col_ref[...]).astype(MXU_DTYPE)
    qm = _dot(wuq_ref[...], cq_n)
    for h in range(MLA_HEADS):
        qh = qm[128 * h:128 * h + 128]
        qm_ref[0, 128 * h:128 * h + 128, :] = (_rope_rows(qh, fcm, fsm, 8) * (MLA_SCALE * LOG2E)).astype(MXU_DTYPE)
    ckv_f = hf[640:768]
    ckv_fn = (ckv_f * _rms_scale(ckv_f, 0) * gkv_col_ref[...]).astype(MXU_DTYPE)
    vtm_ref[0] = _dot(wv_ref[...], ckv_fn).astype(MXU_DTYPE)

    zeros32 = jnp.zeros((DIFF_QK, TM), MXU_DTYPE)
    for c in range(2):
        for rb in range(4):
            r0 = 768 + 128 * c + 32 * rb
            piece = (_rope_rows(hf[r0:r0 + 32], fc32, fs32, 8) * (DIFF_SCALE * LOG2E)).astype(MXU_DTYPE)
            for cb in range(4):
                qd_ref[0, 0, c, 32 * rb:32 * rb + 32, cb * TM:(cb + 1) * TM] = piece if cb == rb else zeros32
    vtd_ref[0] = hf[1024:1280].astype(MXU_DTYPE)

    s_heads = []
    for h in range(4):
        q = hf[1280 + 64 * h:1344 + 64 * h]
        s_heads.append((_rope_rows(q, fc64, fs64, 16) * (GQA_SCALE * LOG2E)).astype(MXU_DTYPE))
    place_heads(qs_ref, s_heads)
    vts_ref[0] = hf[1536:1664].astype(MXU_DTYPE)


def _inproj(xs, modsel, g_pre, w, rope_t, rope_f, *, n_ctx_tiles):
    bsz, t_all, d = xs.shape
    nt = t_all // TM
    bt = lambda t, b: (b, t, 0)
    x_spec = pl.BlockSpec((SPS, TM, d), bt)
    mod_spec = pl.BlockSpec((SPS, 1, 9, d), lambda t, b: (b, jnp.where(t < n_ctx_tiles, 0, 1), 0, 0))
    tok = lambda n: (pl.BlockSpec((SPS, TM, n), bt), jax.ShapeDtypeStruct((bsz, t_all, n), MXU_DTYPE))
    feat = lambda n: (pl.BlockSpec((SPS, n, TM), lambda t, b: (b, 0, t)),
                      jax.ShapeDtypeStruct((bsz, n, t_all), MXU_DTYPE))
    qblk = (pl.BlockSpec((SPS, 1, 128, 4 * TM), lambda t, b: (b, t, 0, 0)),
            jax.ShapeDtypeStruct((bsz, nt, 128, 4 * TM), MXU_DTYPE))
    qdblk = (pl.BlockSpec((SPS, 1, 2, 128, 4 * TM), lambda t, b: (b, t, 0, 0, 0)),
             jax.ShapeDtypeStruct((bsz, nt, 2, 128, 4 * TM), MXU_DTYPE))
    outs = [tok(128), tok(512), tok(256), tok(128), tok(GATE_COLS),
            qblk, feat(128), feat(512), feat(256), qdblk, feat(256), qblk, feat(128)]
    consts = [g_pre, w["w_tok"], w["w_feat_t"], w["w_uq_t"], w["w_k_exp"], w["w_v_t"],
              w["gk_row"], w["gkv_row"], w["gq_col"], w["gmq_col"], w["gkv_col"]]
    batched = (True, True) + (False,) * (len(consts) + 2) + (True,) * len(outs)
    return pl.pallas_call(
        _per_sample(_inproj_kernel, batched),
        grid=(nt, bsz // SPS),
        in_specs=[x_spec, mod_spec] + [_const_spec(c.shape) for c in consts]
        + [pl.BlockSpec((4, TM, 128), lambda t, b: (0, t, 0)),
           pl.BlockSpec((448, TM), lambda t, b: (0, t))],
        out_specs=[o[0] for o in outs],
        out_shape=[o[1] for o in outs],
        compiler_params=_cparams(2),
        name="mixer_inproj",
    )(xs, modsel, *consts, rope_t, rope_f)


def _value_lhs(vt, r0):
    ones = jnp.ones((ONES_ROWS, vt.shape[1]), MXU_DTYPE)
    return jnp.concatenate([vt[r0:r0 + HEAD_DIM, :], ones], axis=0)


def _dense_attn_kernel(*refs, units, tk, n_ctx, n_lat, diff):
    if diff:
        (q1_ref, q2_ref, k_ref, vt_ref, lam_ref, subln_ref, o_ref,
         sa_ref, sb_ref, cma_ref, cmb_ref, m_ref, acc_ref) = refs
    else:
        q1_ref, q2_ref, k_ref, vt_ref, o_ref, sa_ref, sb_ref, cma_ref, cmb_ref, m_ref, acc_ref = refs
    step = pl.program_id(1)
    n_tiles = n_lat // tk
    nu = len(units)
    q_refs = (q1_ref, q2_ref)
    both, second = (0, 1), (1,)
    m_ref[...] = jnp.full(m_ref.shape, NEG_INF, F32)
    acc_ref[...] = jnp.zeros(acc_ref.shape, F32)
    buf_a, buf_b = (sa_ref, cma_ref), (sb_ref, cmb_ref)

    col_blocks = [(qi, u, c0, r0) for qi in both for u, (_, _, blocks) in enumerate(units) for c0, r0 in blocks]

    def scores_block(buf, off, size, qi, u, c0):
        s_ref, cm_ref = buf
        q_idx, k0, _ = units[u]
        s = _dot(k_ref[0, pl.ds(off, size), k0:k0 + 128], q_refs[qi][q_idx][:, c0:c0 + TM])
        s_ref[qi * nu + u, 0:size, c0:c0 + TM] = s
        cm_ref[qi * nu + u, :, c0:c0 + TM] = jnp.max(s, axis=0, keepdims=True)

    def consume_block(buf, off, size, qi, u, c0, r0):
        s_ref, cm_ref = buf
        w = qi * nu + u
        m_old = m_ref[w, :, c0:c0 + TM]
        m_new = jnp.maximum(m_old, cm_ref[w, :, c0:c0 + TM])
        alpha = jnp.exp2(m_old - m_new)
        m_ref[w, :, c0:c0 + TM] = m_new
        p = jnp.exp2(s_ref[w, 0:size, c0:c0 + TM] - m_new).astype(MXU_DTYPE)
        pv = _dot(_value_lhs(vt_ref[0, :, pl.ds(off, size)], r0), p)
        acc_ref[w, :, c0:c0 + TM] = alpha * acc_ref[w, :, c0:c0 + TM] + pv

    def scores(buf, off, size, halves):
        for qi, u, c0, _ in col_blocks:
            if qi in halves:
                scores_block(buf, off, size, qi, u, c0)

    def consume(buf, off, size, halves):
        for qi, u, c0, r0 in col_blocks:
            if qi in halves:
                consume_block(buf, off, size, qi, u, c0, r0)

    def finalize(halves):
        if diff:
            lf = lam_ref[...]
            lam_init = lam_ref[4:5, 0:1]
            lam = (jnp.exp(jnp.sum(lf[0:1, 0:DIFF_QK] * lf[1:2, 0:DIFF_QK], axis=1, keepdims=True))
                   - jnp.exp(jnp.sum(lf[2:3, 0:DIFF_QK] * lf[3:4, 0:DIFF_QK], axis=1, keepdims=True)) + lam_init)
        for qi in halves:
            out0 = qi * TM
            for u, (_, _, blocks) in enumerate(units):
                w = qi * nu + u
                inv = 1.0 / acc_ref[w, HEAD_DIM:HEAD_DIM + 1, :]
                if diff:
                    for hh in range(2):
                        c1, c2 = 2 * hh * TM, (2 * hh + 1) * TM
                        o = (acc_ref[w, 0:HEAD_DIM, c1:c1 + TM] * inv[:, c1:c1 + TM]
                             - lam * (acc_ref[w, 0:HEAD_DIM, c2:c2 + TM] * inv[:, c2:c2 + TM]))
                        o = o * _rms_scale(o, 0) * subln_ref[...] * (1.0 - lam_init)
                        h = 2 * u + hh
                        o_ref[0, HEAD_DIM * h:HEAD_DIM * h + HEAD_DIM, out0:out0 + TM] = o
                else:
                    for c0, _ in blocks:
                        h = (u * len(blocks) * TM + c0) // TM
                        o_ref[0, HEAD_DIM * h:HEAD_DIM * h + HEAD_DIM, out0:out0 + TM] = (
                            acc_ref[w, 0:HEAD_DIM, c0:c0 + TM] * inv[:, c0:c0 + TM])

    def lat(j):
        return n_ctx + j * tk if isinstance(j, int) else pl.multiple_of(n_ctx + j * tk, 128)

    @pl.when(step == 0)
    def _():
        scores(buf_a, 0, n_ctx, second)
        consume(buf_a, 0, n_ctx, second)
        o_ref[0, :, 0:TM] = jnp.zeros((BRANCH_W, TM), F32)
        finalize(second)

    @pl.when(step != 0)
    def _():
        buf_of = lambda parity: buf_a if parity % 2 else buf_b

        def stage(parity, j):
            for qi, u, c0, r0 in col_blocks:
                scores_block(buf_of(parity + 1), lat(j + 1), tk, qi, u, c0)
                consume_block(buf_of(parity), lat(j), tk, qi, u, c0, r0)

        scores(buf_a, 0, n_ctx, both)
        for qi, u, c0, r0 in col_blocks:
            scores_block(buf_b, n_ctx, tk, qi, u, c0)
            consume_block(buf_a, 0, n_ctx, qi, u, c0, r0)
        n_loop = (n_tiles - 1) // STAGES_PER_TRIP

        def body(i, carry):
            for r in range(STAGES_PER_TRIP):
                stage(r, STAGES_PER_TRIP * i + r)
            return carry

        lax.fori_loop(0, n_loop, body, 0)
        for j in range(STAGES_PER_TRIP * n_loop, n_tiles - 1):
            stage(j, j)
        consume(buf_of(n_tiles - 1), lat(n_tiles - 1), tk, both)
        finalize(both)


def _dense_attention(q, k, vt, *, kind, n_ctx, lam_tab=None, subln_col=None):
    bsz, t_all, dk = k.shape
    dv = vt.shape[1]
    n_lat = t_all - n_ctx
    n_steps = 1 + (n_lat // TM) // 2
    tiles = (lambda t: jnp.maximum(2 * t - 1, 0), lambda t: 2 * t)
    if kind == "gqa":
        q_specs = [pl.BlockSpec((1, 1, 128, 4 * TM), lambda b, t, f=f: (b, f(t), 0, 0)) for f in tiles]
        units = (((0, 0), 0, tuple((h * TM, HEAD_DIM * (h // 2)) for h in range(4))),)
        ncols = 4 * TM
    elif kind == "diff":
        q_specs = [pl.BlockSpec((1, 1, 2, 128, 4 * TM), lambda b, t, f=f: (b, f(t), 0, 0, 0)) for f in tiles]
        units = tuple(((0, 0, u), 128 * u,
                       tuple(((2 * hh + mm) * TM, HEAD_DIM * (2 * u + hh)) for hh in range(2) for mm in range(2)))
                      for u in range(2))
        ncols = 4 * TM
    else:
        q_specs = [pl.BlockSpec((1, 512, TM), lambda b, t, f=f: (b, 0, f(t))) for f in tiles]
        units = tuple(((0, slice(128 * h, 128 * h + 128)), 128 * h, ((0, HEAD_DIM * h),)) for h in range(4))
        ncols = TM
    in_specs = q_specs + [pl.BlockSpec((1, t_all, dk), lambda b, t: (b, 0, 0)),
                          pl.BlockSpec((1, dv, t_all), lambda b, t: (b, 0, 0))]
    args = [q, q, k, vt]
    if kind == "diff":
        in_specs += [_const_spec(lam_tab.shape), _const_spec(subln_col.shape)]
        args += [lam_tab, subln_col]
    nw = 2 * len(units)
    return pl.pallas_call(
        functools.partial(_dense_attn_kernel, units=units, tk=TK, n_ctx=n_ctx, n_lat=n_lat,
                          diff=(kind == "diff")),
        grid=(bsz, n_steps),
        in_specs=in_specs,
        out_specs=pl.BlockSpec((1, BRANCH_W, 2 * TM), lambda b, t: (b, 0, t)),
        out_shape=jax.ShapeDtypeStruct((bsz, BRANCH_W, 2 * TM * n_steps), F32),
        scratch_shapes=[pltpu.VMEM((nw, TK, ncols), F32), pltpu.VMEM((nw, TK, ncols), F32),
                        pltpu.VMEM((nw, 1, ncols), F32), pltpu.VMEM((nw, 1, ncols), F32),
                        pltpu.VMEM((nw, 1, ncols), F32), pltpu.VMEM((nw, ACC_ROWS, ncols), F32)],
        compiler_params=_cparams(2),
        name=f"attn_{kind}",
    )(*args)


def _window_attn_kernel(q1_ref, q2_ref, k_ref, vt_ref, sink_ref, o_ref,
                        sc_ref, sw_ref, bias_ref, cmc_ref, cmw_ref, m_ref, acc_ref, *, n_ctx, t_all, wk):
    step = pl.program_id(1)
    q_refs = (q1_ref, q2_ref)
    both, second = (0, 1), (1,)
    m_ref[...] = jnp.broadcast_to(sink_ref[...] * LOG2E, m_ref.shape)
    acc_ref[:, 0:HEAD_DIM, :] = jnp.zeros((2, HEAD_DIM, 4 * TM), F32)
    acc_ref[:, HEAD_DIM:ACC_ROWS, :] = jnp.ones((2, ONES_ROWS, 4 * TM), F32)

    def window(h):
        q0 = (2 * step - 1 + h) * TM
        start = pl.multiple_of(jnp.clip(q0 - WINDOW, n_ctx, t_all - wk), 128)
        kpos = start + lax.broadcasted_iota(jnp.int32, (wk, TM), 0)
        qpos = q0 + lax.broadcasted_iota(jnp.int32, (wk, TM), 1)
        allowed = (jnp.abs(kpos - qpos) <= WINDOW) & (qpos >= n_ctx)
        bias_ref[h] = jnp.where(allowed, 0.0, NEG_INF)
        return start

    def scores_ctx(h, hd):
        c = slice(hd * TM, (hd + 1) * TM)
        s = _dot(k_ref[0, 0:n_ctx, :], q_refs[h][0, 0, :, c])
        sc_ref[h, :, c] = s
        cmc_ref[h, :, c] = jnp.max(s, axis=0, keepdims=True)

    def scores_win(h, hd, start):
        c = slice(hd * TM, (hd + 1) * TM)
        s = _dot(k_ref[0, pl.ds(start, wk), :], q_refs[h][0, 0, :, c]) + bias_ref[h]
        sw_ref[h, :, c] = s
        cmw_ref[h, :, c] = jnp.max(s, axis=0, keepdims=True)

    def consume(h, hd, s, col_max, vt):
        c = slice(hd * TM, (hd + 1) * TM)
        m_old = m_ref[h, :, c]
        m_new = jnp.maximum(m_old, col_max)
        alpha = jnp.exp2(m_old - m_new)
        m_ref[h, :, c] = m_new
        pv = _dot(_value_lhs(vt, HEAD_DIM * (hd // 2)), jnp.exp2(s - m_new).astype(MXU_DTYPE))
        acc_ref[h, :, c] = alpha * acc_ref[h, :, c] + pv

    def consume_ctx(h, hd):
        c = slice(hd * TM, (hd + 1) * TM)
        consume(h, hd, sc_ref[h, :, c], cmc_ref[h, :, c], vt_ref[0, :, 0:n_ctx])

    def finalize(halves):
        for h in halves:
            inv = 1.0 / acc_ref[h, HEAD_DIM:HEAD_DIM + 1, :]
            for hd in range(4):
                c = slice(hd * TM, (hd + 1) * TM)
                o_ref[0, HEAD_DIM * hd:HEAD_DIM * hd + HEAD_DIM, h * TM:(h + 1) * TM] = acc_ref[h, 0:HEAD_DIM, c] * inv[:, c]

    @pl.when(step == 0)
    def _():
        for hd in range(4):
            scores_ctx(1, hd)
        for hd in range(4):
            consume_ctx(1, hd)
        o_ref[0, :, 0:TM] = jnp.zeros((BRANCH_W, TM), F32)
        finalize(second)

    @pl.when(step != 0)
    def _():
        blocks = [(h, hd) for h in both for hd in range(4)]
        wins = [window(h) for h in both]
        for h, hd in blocks:
            scores_ctx(h, hd)
        for h, hd in blocks:
            scores_win(h, hd, wins[h])
            consume_ctx(h, hd)
        for h, hd in blocks:
            c = slice(hd * TM, (hd + 1) * TM)
            consume(h, hd, sw_ref[h, :, c], cmw_ref[h, :, c], vt_ref[0, :, pl.ds(wins[h], wk)])
        finalize(both)


def _window_attention(q, k, vt, sink_row, *, n_ctx):
    bsz, t_all, dk = k.shape
    wk = TM + 2 * WINDOW
    n_steps = 1 + ((t_all - n_ctx) // TM) // 2
    tiles = (lambda t: jnp.maximum(2 * t - 1, 0), lambda t: 2 * t)
    q_specs = [pl.BlockSpec((1, 1, 128, 4 * TM), lambda b, t, f=f: (b, f(t), 0, 0)) for f in tiles]
    return pl.pallas_call(
        functools.partial(_window_attn_kernel, n_ctx=n_ctx, t_all=t_all, wk=wk),
        grid=(bsz, n_steps),
        in_specs=q_specs + [pl.BlockSpec((1, t_all, dk), lambda b, t: (b, 0, 0)),
                            pl.BlockSpec((1, vt.shape[1], t_all), lambda b, t: (b, 0, 0)),
                            _const_spec(sink_row.shape)],
        out_specs=pl.BlockSpec((1, BRANCH_W, 2 * TM), lambda b, t: (b, 0, t)),
        out_shape=jax.ShapeDtypeStruct((bsz, BRANCH_W, 2 * TM * n_steps), F32),
        scratch_shapes=[pltpu.VMEM((2, n_ctx, 4 * TM), F32), pltpu.VMEM((2, wk, 4 * TM), F32),
                        pltpu.VMEM((2, wk, TM), F32),
                        pltpu.VMEM((2, 1, 4 * TM), F32), pltpu.VMEM((2, 1, 4 * TM), F32),
                        pltpu.VMEM((2, 1, 4 * TM), F32), pltpu.VMEM((2, ACC_ROWS, 4 * TM), F32)],
        compiler_params=_cparams(2),
        name="attn_window",
    )(q, q, k, vt, sink_row)


N_MERGE_REFS = 10


def _merge_residual(x_ref, mod_ref, gpost1_ref, oa_ref, om_ref, od_ref, os_ref, g_ref, wb_ref, wo_ref):
    x = x_ref[0]
    gate = mod_ref[0, 0, 5:6, :]
    y = None
    for i, o_t in enumerate((oa_ref, om_ref, od_ref, os_ref)):
        o_tok = o_t[0].T.astype(MXU_DTYPE)
        term = g_ref[0, :, i * D_MODEL:(i + 1) * D_MODEL].astype(F32) * _dot(o_tok, wb_ref[i])
        y = term if y is None else y + term
    z = _dot(y.astype(MXU_DTYPE), wo_ref[...])
    return x + gate * (z * _rms_scale(z, -1) * gpost1_ref[...])


def _merge_ffn_kernel(batched):
    def kernel(*refs):
        views = [[r.at[i:i + 1] if b else r for r, b in zip(refs, batched)] for i in range(SPS)]
        mixed = [_merge_residual(*v[:N_MERGE_REFS]) for v in views]
        for v, x in zip(views, mixed):
            gpre2_ref, gpost2_ref, wg_ref, wu_ref, wd_ref, o_ref = v[N_MERGE_REFS:]
            o_ref[0] = _ffn_residual(x, v[1], gpre2_ref, gpost2_ref, wg_ref, wu_ref, wd_ref, 2)
    return kernel


def _merge_ffn(xs, modsel, g_post1, outs, gates, w_branch, w_out, g_pre2, g_post2, wg, wu, wd,
               *, n_ctx_tiles, latent_out):
    bsz, t_all, d = xs.shape
    first = n_ctx_tiles if latent_out else 0
    x_spec, mod_spec = _token_specs(n_ctx_tiles, first)
    o_spec = pl.BlockSpec((SPS, BRANCH_W, TM), lambda b, t: (b, 0, t + first + 1))
    consts = [g_pre2, g_post2, wg, wu, wd]
    batched = (True, True, False) + (True,) * 5 + (False,) * (2 + len(consts)) + (True,)
    return pl.pallas_call(
        _merge_ffn_kernel(batched),
        grid=(bsz // SPS, t_all // TM - first),
        in_specs=[x_spec, mod_spec, _const_spec((1, d)), o_spec, o_spec, o_spec, o_spec,
                  pl.BlockSpec((SPS, TM, GATE_COLS), lambda b, t: (b, t + first, 0)),
                  _const_spec(w_branch.shape), _const_spec(w_out.shape)] + [_const_spec(c.shape) for c in consts],
        out_specs=pl.BlockSpec((SPS, TM, d), lambda b, t: (b, t, 0)),
        out_shape=jax.ShapeDtypeStruct((bsz, t_all - first * TM, d), F32),
        compiler_params=_cparams(2),
        name="merge_ffn_half_2",
    )(xs, modsel, g_post1, *outs, gates, w_branch, w_out, *consts)


def _rope_tables(n_ctx, n_lat):
    pos = jnp.arange(n_lat)
    row = (pos // GRID_W).astype(F32)
    col = (pos % GRID_W).astype(F32)

    def lane_pattern(rot_dim):
        half = rot_dim // 2
        inv_freq = ROPE_THETA ** (-jnp.arange(0, half, 2, dtype=F32) / half)
        ar, ac = row[:, None] * inv_freq[None, :], col[:, None] * inv_freq[None, :]
        cos = jnp.concatenate([jnp.cos(ar), jnp.cos(ar), jnp.cos(ac), jnp.cos(ac)], axis=1)
        sin = jnp.concatenate([-jnp.sin(ar), jnp.sin(ar), -jnp.sin(ac), jnp.sin(ac)], axis=1)
        cos = jnp.concatenate([jnp.ones((n_ctx, rot_dim), F32), cos], axis=0)
        sin = jnp.concatenate([jnp.zeros((n_ctx, rot_dim), F32), sin], axis=0)
        return cos, sin

    c64, s64 = lane_pattern(HEAD_DIM)
    c32, s32 = lane_pattern(DIFF_QK)
    t_all = n_ctx + n_lat
    rope_t = jnp.stack([jnp.tile(c64, (1, 2)), jnp.tile(s64, (1, 2)), jnp.tile(c32, (1, 4)), jnp.tile(s32, (1, 4))])
    ones, zeros = jnp.ones((MLA_NOPE, t_all), F32), jnp.zeros((MLA_NOPE, t_all), F32)
    pad1, pad0 = jnp.ones((32, t_all), F32), jnp.zeros((32, t_all), F32)
    rope_f = jnp.concatenate([c64.T, s64.T, c32.T, s32.T,
                              ones, c32.T, pad1, zeros, s32.T, pad0], axis=0)
    return rope_t, rope_f


def _layer_weights(l, w_in, gqa_q_norm, gqa_k_norm, mla_q_norm, mla_kv_norm, mla_w_uq, mla_w_ukv):
    w = w_in[l]
    col = lambda a, n: w[:, a:a + n]
    kpe = jnp.zeros((D_MODEL, 128), F32).at[:, 64:96].set(col(M0 + 384, 32))
    w_tok = jnp.concatenate([col(A0 + 256, 128), col(M0 + 256, 128), kpe, col(X0 + 256, 256),
                             col(S0 + 256, 128), col(G0, GATE_COLS)], axis=1)
    w_feat = jnp.concatenate([col(A0, 256), col(A0 + 384, 128), col(M0, 256), col(M0 + 256, 128),
                              col(X0, 256), col(X0 + 512, 256), col(S0, 256), col(S0 + 384, 128)], axis=1)
    uq = mla_w_uq[l].reshape(MLA_Q_RANK, MLA_HEADS, MLA_NOPE + MLA_ROPE)
    uq = jnp.pad(uq, ((0, 0), (0, 0), (0, 128 - MLA_NOPE - MLA_ROPE))).reshape(MLA_Q_RANK, 512)
    ukv = mla_w_ukv[l].reshape(MLA_KV_RANK, MLA_HEADS, 2 * MLA_NOPE)
    k_exp = jnp.pad(ukv[:, :, :MLA_NOPE], ((0, 0), (0, 0), (0, 64))).reshape(MLA_KV_RANK, 512)
    v_t = ukv[:, :, MLA_NOPE:].reshape(MLA_KV_RANK, 256).T
    return {
        "w_tok": w_tok.astype(MXU_DTYPE), "w_feat_t": w_feat.T.astype(MXU_DTYPE),
        "w_uq_t": uq.T.astype(MXU_DTYPE), "w_k_exp": k_exp.astype(MXU_DTYPE), "w_v_t": v_t.astype(MXU_DTYPE),
        "gk_row": jnp.tile(gqa_k_norm[l], 2)[None, :], "gkv_row": mla_kv_norm[l][None, :],
        "gq_col": gqa_q_norm[l][:, None], "gmq_col": mla_q_norm[l][:, None], "gkv_col": mla_kv_norm[l][:, None],
    }


def kernel(x, c, ctx, c_ctx, w_mod, b_mod, g_pre, g_post, w_ffn_gate, w_ffn_up, w_ffn_down, w_in,
           gqa_q_norm, gqa_k_norm, mla_q_norm, mla_kv_norm, mla_w_uq, mla_w_ukv,
           diff_lambda, diff_subln, swa_sink, w_branch, w_out):
    bsz, n_lat, d = x.shape
    n_ctx = ctx.shape[1]
    assert d == D_MODEL and n_ctx == TM and n_lat % (2 * TK) == 0 and bsz + 1 <= MOD_ROWS and bsz % SPS == 0
    n_ctx_tiles = n_ctx // TM

    cvec = jnp.zeros((MOD_ROWS, d), F32).at[:bsz].set(c).at[bsz].set(c_ctx)
    mod = _modulation(cvec, w_mod, b_mod).reshape(DEPTH, MOD_ROWS, 9, d)
    modsel = jnp.stack([jnp.broadcast_to(mod[:, bsz:bsz + 1], (DEPTH, bsz, 9, d)), mod[:, :bsz]], axis=2)

    rope_t, rope_f = _rope_tables(n_ctx, n_lat)
    xs = (ctx, x)
    cast = lambda a: a.astype(MXU_DTYPE)

    for l in range(DEPTH):
        ffn_w = lambda i: (cast(w_ffn_gate[l, i]), cast(w_ffn_up[l, i]), cast(w_ffn_down[l, i]))
        xs = _ffn_half(xs, modsel[l], g_pre[l, 0][None], g_post[l, 0][None], *ffn_w(0),
                       sub=0, n_ctx_tiles=n_ctx_tiles)
        w = _layer_weights(l, w_in, gqa_q_norm, gqa_k_norm, mla_q_norm, mla_kv_norm, mla_w_uq, mla_w_ukv)
        (k_a, k_m, k_d, k_s, gates, q_a, vt_a, q_m, vt_m, q_d, vt_d, q_s, vt_s) = _inproj(
            xs, modsel[l], g_pre[l, 1][None], w, rope_t, rope_f, n_ctx_tiles=n_ctx_tiles)
        lam_init = 0.8 - 0.6 * math.exp(-0.3 * l)
        lam_tab = jnp.zeros((8, 128), F32).at[0:4, 0:DIFF_QK].set(diff_lambda[l]).at[4, :].set(lam_init)
        sink_row = jnp.repeat(swa_sink[l], TM)[None, :]
        outs = (_dense_attention(q_a, k_a, vt_a, kind="gqa", n_ctx=n_ctx),
                _dense_attention(q_m, k_m, vt_m, kind="mla", n_ctx=n_ctx),
                _dense_attention(q_d, k_d, vt_d, kind="diff", n_ctx=n_ctx,
                                 lam_tab=lam_tab, subln_col=diff_subln[l][:, None]),
                _window_attention(q_s, k_s, vt_s, sink_row, n_ctx=n_ctx))
        xs = _merge_ffn(xs, modsel[l], g_post[l, 1][None], outs, gates, cast(w_branch[l]), cast(w_out[l]),
                        g_pre[l, 2][None], g_post[l, 2][None], *ffn_w(1), n_ctx_tiles=n_ctx_tiles,
                        latent_out=(l == DEPTH - 1))
    return xs
```

```python
import functools
import math

import jax
import jax.numpy as jnp
from jax import lax
from jax.experimental import pallas as pl
from jax.experimental.pallas import tpu as pltpu

MXU_DTYPE = jnp.bfloat16
F32 = jnp.float32

D_MODEL = 1024
D_FF = 2816
DEPTH = 2
GRID_W = 64
WINDOW = 128
ROPE_THETA = 10000.0
EPS = 1e-6
NEG_INF = -1e30
HEAD_DIM = 64
MLA_HEADS = 4
MLA_Q_RANK = 256
MLA_KV_RANK = 128
MLA_NOPE = 64
MLA_ROPE = 32
DIFF_QK = 32
N_BRANCH = 4
BRANCH_W = 256
GQA_SCALE = HEAD_DIM ** -0.5
MLA_SCALE = (MLA_NOPE + MLA_ROPE) ** -0.5
DIFF_SCALE = DIFF_QK ** -0.5
LOG2E = math.log2(math.e)
ONES_ROWS = 16
ACC_ROWS = HEAD_DIM + ONES_ROWS

A0, M0, X0, S0, G0 = 0, 512, 928, 1696, 2208
GATE_COLS = N_BRANCH * D_MODEL
TOK_COLS = 768
FEAT_ROWS = 1664

TM = 256
SPS = 2
TK = 256
STAGES_PER_TRIP = 8
MOD_ROWS = 16
MOD_TN = 1024
V7X_VMEM_LIMIT = 56 * 1024 * 1024


def _cparams(n_axes):
    return pltpu.CompilerParams(dimension_semantics=("arbitrary",) * n_axes,
                                vmem_limit_bytes=V7X_VMEM_LIMIT)


def _const_spec(shape):
    n = len(shape)
    return pl.BlockSpec(shape, lambda *_: (0,) * n, pipeline_mode=pl.Buffered(1))


def _per_sample(body, batched):
    def kernel(*refs):
        for i in range(SPS):
            body(*[r.at[i:i + 1] if b else r for r, b in zip(refs, batched)])
    return kernel


def _sigmoid(v):
    return 1.0 / (1.0 + jnp.exp(-v))


def _dot(a, b):
    return jnp.dot(a, b, preferred_element_type=F32)


def _rms_scale(v, axis):
    return lax.rsqrt(jnp.mean(v * v, axis=axis, keepdims=True) + EPS)


def _mod_kernel(c_ref, w_ref, b_ref, o_ref):
    c = c_ref[...]
    s = (c * _sigmoid(c)).astype(MXU_DTYPE)
    o_ref[0] = _dot(s, w_ref[0].astype(MXU_DTYPE)) + b_ref[0]


def _modulation(cvec, w_mod, b_mod):
    depth, d, n = w_mod.shape
    return pl.pallas_call(
        _mod_kernel,
        grid=(depth, n // MOD_TN),
        in_specs=[pl.BlockSpec((MOD_ROWS, d), lambda l, j: (0, 0)),
                  pl.BlockSpec((1, d, MOD_TN), lambda l, j: (l, 0, j)),
                  pl.BlockSpec((1, 1, MOD_TN), lambda l, j: (l, 0, j))],
        out_specs=pl.BlockSpec((1, MOD_ROWS, MOD_TN), lambda l, j: (l, 0, j)),
        out_shape=jax.ShapeDtypeStruct((depth, MOD_ROWS, n), F32),
        compiler_params=_cparams(2),
        name="modulation",
    )(cvec, w_mod, b_mod.reshape(depth, 1, n))


def _adaln(x, g_row, shift, scale):
    return (x * _rms_scale(x, -1) * g_row) * (1.0 + scale) + shift


def _ffn_residual(x, mod_ref, gpre_ref, gpost_ref, wg_ref, wu_ref, wd_ref, sub):
    shift = mod_ref[0, 0, 3 * sub:3 * sub + 1, :]
    scale = mod_ref[0, 0, 3 * sub + 1:3 * sub + 2, :]
    gate = mod_ref[0, 0, 3 * sub + 2:3 * sub + 3, :]
    u = _adaln(x, gpre_ref[...], shift, scale).astype(MXU_DTYPE)
    a = _dot(u, wg_ref[...])
    b = _dot(u, wu_ref[...])
    h = ((a * _sigmoid(a)) * b).astype(MXU_DTYPE)
    y = _dot(h, wd_ref[...])
    yn = y * _rms_scale(y, -1) * gpost_ref[...]
    return x + 0.5 * gate * yn


def _ffn_kernel(*refs, sub, n_ctx_tiles, split_in):
    if split_in:
        ctx_ref, lat_ref, mod_ref, gpre_ref, gpost_ref, wg_ref, wu_ref, wd_ref, o_ref = refs
        x = jnp.where(pl.program_id(1) < n_ctx_tiles, ctx_ref[0], lat_ref[0])
    else:
        x_ref, mod_ref, gpre_ref, gpost_ref, wg_ref, wu_ref, wd_ref, o_ref = refs
        x = x_ref[0]
    o_ref[0] = _ffn_residual(x, mod_ref, gpre_ref, gpost_ref, wg_ref, wu_ref, wd_ref, sub)


def _token_specs(n_ctx_tiles, first=0):
    x_spec = pl.BlockSpec((SPS, TM, D_MODEL), lambda b, t: (b, t + first, 0))
    mod_spec = pl.BlockSpec((SPS, 1, 9, D_MODEL),
                            lambda b, t: (b, jnp.where(t + first < n_ctx_tiles, 0, 1), 0, 0))
    return x_spec, mod_spec


def _latent_spec(n_ctx_tiles):
    return pl.BlockSpec((SPS, TM, D_MODEL), lambda b, t: (b, jnp.maximum(t - n_ctx_tiles, 0), 0))


def _ffn_half(xs, modsel, g_pre, g_post, wg, wu, wd, *, sub, n_ctx_tiles):
    split_in = isinstance(xs, tuple)
    x_spec, mod_spec = _token_specs(n_ctx_tiles)
    if split_in:
        ctx, lat = xs
        bsz, t_all, d = lat.shape[0], ctx.shape[1] + lat.shape[1], lat.shape[2]
        x_specs = [pl.BlockSpec((SPS, TM, d), lambda b, t: (b, jnp.minimum(t, n_ctx_tiles - 1), 0)),
                   _latent_spec(n_ctx_tiles)]
        x_args = [ctx, lat]
    else:
        bsz, t_all, d = xs.shape
        x_specs, x_args = [x_spec], [xs]
    batched = (True,) * (len(x_args) + 1) + (False,) * 5 + (True,)
    return pl.pallas_call(
        _per_sample(functools.partial(_ffn_kernel, sub=sub, n_ctx_tiles=n_ctx_tiles, split_in=split_in), batched),
        grid=(bsz // SPS, t_all // TM),
        in_specs=x_specs + [mod_spec, _const_spec((1, d)), _const_spec((1, d)),
                            _const_spec(wg.shape), _const_spec(wu.shape), _const_spec(wd.shape)],
        out_specs=x_spec,
        out_shape=jax.ShapeDtypeStruct((bsz, t_all, d), F32),
        compiler_params=_cparams(2),
        name=f"ffn_half_{sub}",
    )(*x_args, modsel, g_pre, g_post, wg, wu, wd)


def _rope_lanes(v, cos, sin, pair):
    width = v.shape[1]
    lane = lax.broadcasted_iota(jnp.int32, v.shape, 1)
    first = (lane & pair) == 0
    partner = jnp.where(first, pltpu.roll(v, width - pair, 1), pltpu.roll(v, pair, 1))
    return v * cos + partner * sin


def _swap_row_blocks(v, pair):
    parts = []
    for r in range(0, v.shape[0], 2 * pair):
        parts.append(v[r + pair:r + 2 * pair])
        parts.append(v[r:r + pair])
    return jnp.concatenate(parts, axis=0)


def _rope_rows(v, cos, sin, pair):
    return v * cos + _swap_row_blocks(v, pair) * sin


def _inproj_kernel(x_ref, mod_ref, gpre_ref, wtok_ref, wft_ref, wuq_ref, wkx_ref, wv_ref,
                   gk_ref, gkv_row_ref, gq_col_ref, gmq_col_ref, gkv_col_ref,
                   rope_t_ref, rope_f_ref,
                   ka_ref, km_ref, kd_ref, ks_ref, g_ref,
                   qa_ref, vta_ref, qm_ref, vtm_ref, qd_ref, vtd_ref, qs_ref, vts_ref):
    x = x_ref[0]
    shift = mod_ref[0, 0, 3:4, :]
    scale = mod_ref[0, 0, 4:5, :]
    u = _adaln(x, gpre_ref[...], shift, scale).astype(MXU_DTYPE)

    c64, s64, c32, s32 = rope_t_ref[0], rope_t_ref[1], rope_t_ref[2], rope_t_ref[3]

    ht = _dot(u, wtok_ref[:, 0:TOK_COLS])
    k_a = ht[:, 0:128]
    sq = k_a * k_a
    lane = lax.broadcasted_iota(jnp.int32, sq.shape, 1)
    low = lane < HEAD_DIM
    ss_lo = jnp.sum(jnp.where(low, sq, 0.0), axis=1, keepdims=True)
    ss_hi = jnp.sum(jnp.where(low, 0.0, sq), axis=1, keepdims=True)
    ms = jnp.where(low, ss_lo, ss_hi) * (1.0 / HEAD_DIM)
    k_a = k_a * lax.rsqrt(ms + EPS) * gk_ref[...]
    ka_ref[0] = _rope_lanes(k_a, c64, s64, 16).astype(MXU_DTYPE)

    ckv = ht[:, 128:256]
    ckv_n = (ckv * _rms_scale(ckv, -1) * gkv_row_ref[...]).astype(MXU_DTYPE)
    k_nope = _dot(ckv_n, wkx_ref[...])
    k_pe = _rope_lanes(ht[:, 256:384], c32, s32, 8)
    for h in range(MLA_HEADS):
        km_ref[0, :, 128 * h:128 * h + 128] = (k_nope[:, 128 * h:128 * h + 128] + k_pe).astype(MXU_DTYPE)

    for j in range(2):
        kd_ref[0, :, 128 * j:128 * j + 128] = _rope_lanes(
            ht[:, 384 + 128 * j:512 + 128 * j], c32, s32, 8).astype(MXU_DTYPE)
    ks_ref[0] = _rope_lanes(ht[:, 640:768], c64, s64, 16).astype(MXU_DTYPE)

    for j in range(GATE_COLS // D_MODEL):
        lo = TOK_COLS + j * D_MODEL
        hg = _dot(u, wtok_ref[:, lo:lo + D_MODEL])
        g_ref[0, :, j * D_MODEL:(j + 1) * D_MODEL] = _sigmoid(hg).astype(g_ref.dtype)

    hf = lax.dot_general(wft_ref[...], u, (((1,), (1,)), ((), ())), preferred_element_type=F32)
    fc64, fs64 = rope_f_ref[0:64, :], rope_f_ref[64:128, :]
    fc32, fs32 = rope_f_ref[128:160, :], rope_f_ref[160:192, :]
    fcm, fsm = rope_f_ref[192:320, :], rope_f_ref[320:448, :]
    zeros64 = jnp.zeros((HEAD_DIM, TM), MXU_DTYPE)

    def place_heads(ref, heads):
        for h, q in enumerate(heads):
            r0 = HEAD_DIM * (h // 2)
            ref[0, 0, r0:r0 + HEAD_DIM, h * TM:(h + 1) * TM] = q
            z0 = HEAD_DIM - r0
            ref[0, 0, z0:z0 + HEAD_DIM, h * TM:(h + 1) * TM] = zeros64

    q_heads = []
    for h in range(4):
        q = hf[64 * h:64 * h + 64]
        q = q * _rms_scale(q, 0) * gq_col_ref[...]
        q_heads.append((_rope_rows(q, fc64, fs64, 16) * (GQA_SCALE * LOG2E)).astype(MXU_DTYPE))
    place_heads(qa_ref, q_heads)
    vta_ref[0] = hf[256:384].astype(MXU_DTYPE)

    cq = hf[384:640]
    cq_n = (cq * _rms_scale(cq, 0) * gmq_col_ref[...]).astype(MXU_DTYPE)
    qm = _dot(wuq_ref[...], cq_n)
    for h in range(MLA_HEADS):
        qh = qm[128 * h:128 * h + 128]
        qm_ref[0, 128 * h:128 * h + 128, :] = (_rope_rows(qh, fcm, fsm, 8) * (MLA_SCALE * LOG2E)).astype(MXU_DTYPE)
    ckv_f = hf[640:768]
    ckv_fn = (ckv_f * _rms_scale(ckv_f, 0) * gkv_col_ref[...]).astype(MXU_DTYPE)
    vtm_ref[0] = _dot(wv_ref[...], ckv_fn).astype(MXU_DTYPE)

    zeros32 = jnp.zeros((DIFF_QK, TM), MXU_DTYPE)
    for c in range(2):
        for rb in range(4):
            r0 = 768 + 128 * c + 32 * rb
            piece = (_rope_rows(hf[r0:r0 + 32], fc32, fs32, 8) * (DIFF_SCALE * LOG2E)).astype(MXU_DTYPE)
            for cb in range(4):
                qd_ref[0, 0, c, 32 * rb:32 * rb + 32, cb * TM:(cb + 1) * TM] = piece if cb == rb else zeros32
    vtd_ref[0] = hf[1024:1280].astype(MXU_DTYPE)

    s_heads = []
    for h in range(4):
        q = hf[1280 + 64 * h:1344 + 64 * h]
        s_heads.append((_rope_rows(q, fc64, fs64, 16) * (GQA_SCALE * LOG2E)).astype(MXU_DTYPE))
    place_heads(qs_ref, s_heads)
    vts_ref[0] = hf[1536:1664].astype(MXU_DTYPE)


def _inproj(xs, modsel, g_pre, w, rope_t, rope_f, *, n_ctx_tiles):
    bsz, t_all, d = xs.shape
    nt = t_all // TM
    bt = lambda t, b: (b, t, 0)
    x_spec = pl.BlockSpec((SPS, TM, d), bt)
    mod_spec = pl.BlockSpec((SPS, 1, 9, d), lambda t, b: (b, jnp.where(t < n_ctx_tiles, 0, 1), 0, 0))
    tok = lambda n: (pl.BlockSpec((SPS, TM, n), bt), jax.ShapeDtypeStruct((bsz, t_all, n), MXU_DTYPE))
    feat = lambda n: (pl.BlockSpec((SPS, n, TM), lambda t, b: (b, 0, t)),
                      jax.ShapeDtypeStruct((bsz, n, t_all), MXU_DTYPE))
    qblk = (pl.BlockSpec((SPS, 1, 128, 4 * TM), lambda t, b: (b, t, 0, 0)),
            jax.ShapeDtypeStruct((bsz, nt, 128, 4 * TM), MXU_DTYPE))
    qdblk = (pl.BlockSpec((SPS, 1, 2, 128, 4 * TM), lambda t, b: (b, t, 0, 0, 0)),
             jax.ShapeDtypeStruct((bsz, nt, 2, 128, 4 * TM), MXU_DTYPE))
    outs = [tok(128), tok(512), tok(256), tok(128), tok(GATE_COLS),
            qblk, feat(128), feat(512), feat(256), qdblk, feat(256), qblk, feat(128)]
    consts = [g_pre, w["w_tok"], w["w_feat_t"], w["w_uq_t"], w["w_k_exp"], w["w_v_t"],
              w["gk_row"], w["gkv_row"], w["gq_col"], w["gmq_col"], w["gkv_col"]]
    batched = (True, True) + (False,) * (len(consts) + 2) + (True,) * len(outs)
    return pl.pallas_call(
        _per_sample(_inproj_kernel, batched),
        grid=(nt, bsz // SPS),
        in_specs=[x_spec, mod_spec] + [_const_spec(c.shape) for c in consts]
        + [pl.BlockSpec((4, TM, 128), lambda t, b: (0, t, 0)),
           pl.BlockSpec((448, TM), lambda t, b: (0, t))],
        out_specs=[o[0] for o in outs],
        out_shape=[o[1] for o in outs],
        compiler_params=_cparams(2),
        name="mixer_inproj",
    )(xs, modsel, *consts, rope_t, rope_f)


def _value_lhs(vt, r0):
    ones = jnp.ones((ONES_ROWS, vt.shape[1]), MXU_DTYPE)
    return jnp.concatenate([vt[r0:r0 + HEAD_DIM, :], ones], axis=0)


def _dense_attn_kernel(*refs, units, tk, n_ctx, n_lat, diff):
    if diff:
        (q1_ref, q2_ref, k_ref, vt_ref, lam_ref, subln_ref, o_ref,
         sa_ref, sb_ref, cma_ref, cmb_ref, m_ref, acc_ref) = refs
    else:
        q1_ref, q2_ref, k_ref, vt_ref, o_ref, sa_ref, sb_ref, cma_ref, cmb_ref, m_ref, acc_ref = refs
    step = pl.program_id(1)
    n_tiles = n_lat // tk
    nu = len(units)
    q_refs = (q1_ref, q2_ref)
    both, second = (0, 1), (1,)
    m_ref[...] = jnp.full(m_ref.shape, NEG_INF, F32)
    acc_ref[...] = jnp.zeros(acc_ref.shape, F32)
    buf_a, buf_b = (sa_ref, cma_ref), (sb_ref, cmb_ref)

    col_blocks = [(qi, u, c0, r0) for qi in both for u, (_, _, blocks) in enumerate(units) for c0, r0 in blocks]

    def scores_block(buf, off, size, qi, u, c0):
        s_ref, cm_ref = buf
        q_idx, k0, _ = units[u]
        s = _dot(k_ref[0, pl.ds(off, size), k0:k0 + 128], q_refs[qi][q_idx][:, c0:c0 + TM])
        s_ref[qi * nu + u, 0:size, c0:c0 + TM] = s
        cm_ref[qi * nu + u, :, c0:c0 + TM] = jnp.max(s, axis=0, keepdims=True)

    def consume_block(buf, off, size, qi, u, c0, r0):
        s_ref, cm_ref = buf
        w = qi * nu + u
        m_old = m_ref[w, :, c0:c0 + TM]
        m_new = jnp.maximum(m_old, cm_ref[w, :, c0:c0 + TM])
        alpha = jnp.exp2(m_old - m_new)
        m_ref[w, :, c0:c0 + TM] = m_new
        p = jnp.exp2(s_ref[w, 0:size, c0:c0 + TM] - m_new).astype(MXU_DTYPE)
        pv = _dot(_value_lhs(vt_ref[0, :, pl.ds(off, size)], r0), p)
        acc_ref[w, :, c0:c0 + TM] = alpha * acc_ref[w, :, c0:c0 + TM] + pv

    def scores(buf, off, size, halves):
        for qi, u, c0, _ in col_blocks:
            if qi in halves:
                scores_block(buf, off, size, qi, u, c0)

    def consume(buf, off, size, halves):
        for qi, u, c0, r0 in col_blocks:
            if qi in halves:
                consume_block(buf, off, size, qi, u, c0, r0)

    def finalize(halves):
        if diff:
            lf = lam_ref[...]
            lam_init = lam_ref[4:5, 0:1]
            lam = (jnp.exp(jnp.sum(lf[0:1, 0:DIFF_QK] * lf[1:2, 0:DIFF_QK], axis=1, keepdims=True))
                   - jnp.exp(jnp.sum(lf[2:3, 0:DIFF_QK] * lf[3:4, 0:DIFF_QK], axis=1, keepdims=True)) + lam_init)
        for qi in halves:
            out0 = qi * TM
            for u, (_, _, blocks) in enumerate(units):
                w = qi * nu + u
                inv = 1.0 / acc_ref[w, HEAD_DIM:HEAD_DIM + 1, :]
                if diff:
                    for hh in range(2):
                        c1, c2 = 2 * hh * TM, (2 * hh + 1) * TM
                        o = (acc_ref[w, 0:HEAD_DIM, c1:c1 + TM] * inv[:, c1:c1 + TM]
                             - lam * (acc_ref[w, 0:HEAD_DIM, c2:c2 + TM] * inv[:, c2:c2 + TM]))
                        o = o * _rms_scale(o, 0) * subln_ref[...] * (1.0 - lam_init)
                        h = 2 * u + hh
                        o_ref[0, HEAD_DIM * h:HEAD_DIM * h + HEAD_DIM, out0:out0 + TM] = o
                else:
                    for c0, _ in blocks:
                        h = (u * len(blocks) * TM + c0) // TM
                        o_ref[0, HEAD_DIM * h:HEAD_DIM * h + HEAD_DIM, out0:out0 + TM] = (
                            acc_ref[w, 0:HEAD_DIM, c0:c0 + TM] * inv[:, c0:c0 + TM])

    def lat(j):
        return n_ctx + j * tk if isinstance(j, int) else pl.multiple_of(n_ctx + j * tk, 128)

    @pl.when(step == 0)
    def _():
        scores(buf_a, 0, n_ctx, second)
        consume(buf_a, 0, n_ctx, second)
        o_ref[0, :, 0:TM] = jnp.zeros((BRANCH_W, TM), F32)
        finalize(second)

    @pl.when(step != 0)
    def _():
        buf_of = lambda parity: buf_a if parity % 2 else buf_b

        def stage(parity, j):
            for qi, u, c0, r0 in col_blocks:
                scores_block(buf_of(parity + 1), lat(j + 1), tk, qi, u, c0)
                consume_block(buf_of(parity), lat(j), tk, qi, u, c0, r0)

        scores(buf_a, 0, n_ctx, both)
        for qi, u, c0, r0 in col_blocks:
            scores_block(buf_b, n_ctx, tk, qi, u, c0)
            consume_block(buf_a, 0, n_ctx, qi, u, c0, r0)
        n_loop = (n_tiles - 1) // STAGES_PER_TRIP

        def body(i, carry):
            for r in range(STAGES_PER_TRIP):
                stage(r, STAGES_PER_TRIP * i + r)
            return carry

        lax.fori_loop(0, n_loop, body, 0)
        for j in range(STAGES_PER_TRIP * n_loop, n_tiles - 1):
            stage(j, j)
        consume(buf_of(n_tiles - 1), lat(n_tiles - 1), tk, both)
        finalize(both)


def _dense_attention(q, k, vt, *, kind, n_ctx, lam_tab=None, subln_col=None):
    bsz, t_all, dk = k.shape
    dv = vt.shape[1]
    n_lat = t_all - n_ctx
    n_steps = 1 + (n_lat // TM) // 2
    tiles = (lambda t: jnp.maximum(2 * t - 1, 0), lambda t: 2 * t)
    if kind == "gqa":
        q_specs = [pl.BlockSpec((1, 1, 128, 4 * TM), lambda b, t, f=f: (b, f(t), 0, 0)) for f in tiles]
        units = (((0, 0), 0, tuple((h * TM, HEAD_DIM * (h // 2)) for h in range(4))),)
        ncols = 4 * TM
    elif kind == "diff":
        q_specs = [pl.BlockSpec((1, 1, 2, 128, 4 * TM), lambda b, t, f=f: (b, f(t), 0, 0, 0)) for f in tiles]
        units = tuple(((0, 0, u), 128 * u,
                       tuple(((2 * hh + mm) * TM, HEAD_DIM * (2 * u + hh)) for hh in range(2) for mm in range(2)))
                      for u in range(2))
        ncols = 4 * TM
    else:
        q_specs = [pl.BlockSpec((1, 512, TM), lambda b, t, f=f: (b, 0, f(t))) for f in tiles]
        units = tuple(((0, slice(128 * h, 128 * h + 128)), 128 * h, ((0, HEAD_DIM * h),)) for h in range(4))
        ncols = TM
    in_specs = q_specs + [pl.BlockSpec((1, t_all, dk), lambda b, t: (b, 0, 0)),
                          pl.BlockSpec((1, dv, t_all), lambda b, t: (b, 0, 0))]
    args = [q, q, k, vt]
    if kind == "diff":
        in_specs += [_const_spec(lam_tab.shape), _const_spec(subln_col.shape)]
        args += [lam_tab, subln_col]
    nw = 2 * len(units)
    return pl.pallas_call(
        functools.partial(_dense_attn_kernel, units=units, tk=TK, n_ctx=n_ctx, n_lat=n_lat,
                          diff=(kind == "diff")),
        grid=(bsz, n_steps),
        in_specs=in_specs,
        out_specs=pl.BlockSpec((1, BRANCH_W, 2 * TM), lambda b, t: (b, 0, t)),
        out_shape=jax.ShapeDtypeStruct((bsz, BRANCH_W, 2 * TM * n_steps), F32),
        scratch_shapes=[pltpu.VMEM((nw, TK, ncols), F32), pltpu.VMEM((nw, TK, ncols), F32),
                        pltpu.VMEM((nw, 1, ncols), F32), pltpu.VMEM((nw, 1, ncols), F32),
                        pltpu.VMEM((nw, 1, ncols), F32), pltpu.VMEM((nw, ACC_ROWS, ncols), F32)],
        compiler_params=_cparams(2),
        name=f"attn_{kind}",
    )(*args)


def _window_attn_kernel(q1_ref, q2_ref, k_ref, vt_ref, sink_ref, o_ref,
                        sc_ref, sw_ref, bias_ref, cmc_ref, cmw_ref, m_ref, acc_ref, *, n_ctx, t_all, wk):
    step = pl.program_id(1)
    q_refs = (q1_ref, q2_ref)
    both, second = (0, 1), (1,)
    m_ref[...] = jnp.broadcast_to(sink_ref[...] * LOG2E, m_ref.shape)
    acc_ref[:, 0:HEAD_DIM, :] = jnp.zeros((2, HEAD_DIM, 4 * TM), F32)
    acc_ref[:, HEAD_DIM:ACC_ROWS, :] = jnp.ones((2, ONES_ROWS, 4 * TM), F32)

    def window(h):
        q0 = (2 * step - 1 + h) * TM
        start = pl.multiple_of(jnp.clip(q0 - WINDOW, n_ctx, t_all - wk), 128)
        kpos = start + lax.broadcasted_iota(jnp.int32, (wk, TM), 0)
        qpos = q0 + lax.broadcasted_iota(jnp.int32, (wk, TM), 1)
        allowed = (jnp.abs(kpos - qpos) <= WINDOW) & (qpos >= n_ctx)
        bias_ref[h] = jnp.where(allowed, 0.0, NEG_INF)
        return start

    def scores_ctx(h, hd):
        c = slice(hd * TM, (hd + 1) * TM)
        s = _dot(k_ref[0, 0:n_ctx, :], q_refs[h][0, 0, :, c])
        sc_ref[h, :, c] = s
        cmc_ref[h, :, c] = jnp.max(s, axis=0, keepdims=True)

    def scores_win(h, hd, start):
        c = slice(hd * TM, (hd + 1) * TM)
        s = _dot(k_ref[0, pl.ds(start, wk), :], q_refs[h][0, 0, :, c]) + bias_ref[h]
        sw_ref[h, :, c] = s
        cmw_ref[h, :, c] = jnp.max(s, axis=0, keepdims=True)

    def consume(h, hd, s, col_max, vt):
        c = slice(hd * TM, (hd + 1) * TM)
        m_old = m_ref[h, :, c]
        m_new = jnp.maximum(m_old, col_max)
        alpha = jnp.exp2(m_old - m_new)
        m_ref[h, :, c] = m_new
        pv = _dot(_value_lhs(vt, HEAD_DIM * (hd // 2)), jnp.exp2(s - m_new).astype(MXU_DTYPE))
        acc_ref[h, :, c] = alpha * acc_ref[h, :, c] + pv

    def consume_ctx(h, hd):
        c = slice(hd * TM, (hd + 1) * TM)
        consume(h, hd, sc_ref[h, :, c], cmc_ref[h, :, c], vt_ref[0, :, 0:n_ctx])

    def finalize(halves):
        for h in halves:
            inv = 1.0 / acc_ref[h, HEAD_DIM:HEAD_DIM + 1, :]
            for hd in range(4):
                c = slice(hd * TM, (hd + 1) * TM)
                o_ref[0, HEAD_DIM * hd:HEAD_DIM * hd + HEAD_DIM, h * TM:(h + 1) * TM] = acc_ref[h, 0:HEAD_DIM, c] * inv[:, c]

    @pl.when(step == 0)
    def _():
        for hd in range(4):
            scores_ctx(1, hd)
        for hd in range(4):
            consume_ctx(1, hd)
        o_ref[0, :, 0:TM] = jnp.zeros((BRANCH_W, TM), F32)
        finalize(second)

    @pl.when(step != 0)
    def _():
        blocks = [(h, hd) for h in both for hd in range(4)]
        wins = [window(h) for h in both]
        for h, hd in blocks:
            scores_ctx(h, hd)
        for h, hd in blocks:
            scores_win(h, hd, wins[h])
            consume_ctx(h, hd)
        for h, hd in blocks:
            c = slice(hd * TM, (hd + 1) * TM)
            consume(h, hd, sw_ref[h, :, c], cmw_ref[h, :, c], vt_ref[0, :, pl.ds(wins[h], wk)])
        finalize(both)


def _window_attention(q, k, vt, sink_row, *, n_ctx):
    bsz, t_all, dk = k.shape
    wk = TM + 2 * WINDOW
    n_steps = 1 + ((t_all - n_ctx) // TM) // 2
    tiles = (lambda t: jnp.maximum(2 * t - 1, 0), lambda t: 2 * t)
    q_specs = [pl.BlockSpec((1, 1, 128, 4 * TM), lambda b, t, f=f: (b, f(t), 0, 0)) for f in tiles]
    return pl.pallas_call(
        functools.partial(_window_attn_kernel, n_ctx=n_ctx, t_all=t_all, wk=wk),
        grid=(bsz, n_steps),
        in_specs=q_specs + [pl.BlockSpec((1, t_all, dk), lambda b, t: (b, 0, 0)),
                            pl.BlockSpec((1, vt.shape[1], t_all), lambda b, t: (b, 0, 0)),
                            _const_spec(sink_row.shape)],
        out_specs=pl.BlockSpec((1, BRANCH_W, 2 * TM), lambda b, t: (b, 0, t)),
        out_shape=jax.ShapeDtypeStruct((bsz, BRANCH_W, 2 * TM * n_steps), F32),
        scratch_shapes=[pltpu.VMEM((2, n_ctx, 4 * TM), F32), pltpu.VMEM((2, wk, 4 * TM), F32),
                        pltpu.VMEM((2, wk, TM), F32),
                        pltpu.VMEM((2, 1, 4 * TM), F32), pltpu.VMEM((2, 1, 4 * TM), F32),
                        pltpu.VMEM((2, 1, 4 * TM), F32), pltpu.VMEM((2, ACC_ROWS, 4 * TM), F32)],
        compiler_params=_cparams(2),
        name="attn_window",
    )(q, q, k, vt, sink_row)


N_MERGE_REFS = 10


def _merge_residual(x_ref, mod_ref, gpost1_ref, oa_ref, om_ref, od_ref, os_ref, g_ref, wb_ref, wo_ref):
    x = x_ref[0]
    gate = mod_ref[0, 0, 5:6, :]
    y = None
    for i, o_t in enumerate((oa_ref, om_ref, od_ref, os_ref)):
        o_tok = o_t[0].T.astype(MXU_DTYPE)
        term = g_ref[0, :, i * D_MODEL:(i + 1) * D_MODEL].astype(F32) * _dot(o_tok, wb_ref[i])
        y = term if y is None else y + term
    z = _dot(y.astype(MXU_DTYPE), wo_ref[...])
    return x + gate * (z * _rms_scale(z, -1) * gpost1_ref[...])


def _merge_ffn_kernel(batched):
    def kernel(*refs):
        views = [[r.at[i:i + 1] if b else r for r, b in zip(refs, batched)] for i in range(SPS)]
        mixed = [_merge_residual(*v[:N_MERGE_REFS]) for v in views]
        for v, x in zip(views, mixed):
            gpre2_ref, gpost2_ref, wg_ref, wu_ref, wd_ref, o_ref = v[N_MERGE_REFS:]
            o_ref[0] = _ffn_residual(x, v[1], gpre2_ref, gpost2_ref, wg_ref, wu_ref, wd_ref, 2)
    return kernel


def _merge_ffn(xs, modsel, g_post1, outs, gates, w_branch, w_out, g_pre2, g_post2, wg, wu, wd,
               *, n_ctx_tiles, latent_out):
    bsz, t_all, d = xs.shape
    first = n_ctx_tiles if latent_out else 0
    x_spec, mod_spec = _token_specs(n_ctx_tiles, first)
    o_spec = pl.BlockSpec((SPS, BRANCH_W, TM), lambda b, t: (b, 0, t + first + 1))
    consts = [g_pre2, g_post2, wg, wu, wd]
    batched = (True, True, False) + (True,) * 5 + (False,) * (2 + len(consts)) + (True,)
    return pl.pallas_call(
        _merge_ffn_kernel(batched),
        grid=(bsz // SPS, t_all // TM - first),
        in_specs=[x_spec, mod_spec, _const_spec((1, d)), o_spec, o_spec, o_spec, o_spec,
                  pl.BlockSpec((SPS, TM, GATE_COLS), lambda b, t: (b, t + first, 0)),
                  _const_spec(w_branch.shape), _const_spec(w_out.shape)] + [_const_spec(c.shape) for c in consts],
        out_specs=pl.BlockSpec((SPS, TM, d), lambda b, t: (b, t, 0)),
        out_shape=jax.ShapeDtypeStruct((bsz, t_all - first * TM, d), F32),
        compiler_params=_cparams(2),
        name="merge_ffn_half_2",
    )(xs, modsel, g_post1, *outs, gates, w_branch, w_out, *consts)


def _rope_tables(n_ctx, n_lat):
    pos = jnp.arange(n_lat)
    row = (pos // GRID_W).astype(F32)
    col = (pos % GRID_W).astype(F32)

    def lane_pattern(rot_dim):
        half = rot_dim // 2
        inv_freq = ROPE_THETA ** (-jnp.arange(0, half, 2, dtype=F32) / half)
        ar, ac = row[:, None] * inv_freq[None, :], col[:, None] * inv_freq[None, :]
        cos = jnp.concatenate([jnp.cos(ar), jnp.cos(ar), jnp.cos(ac), jnp.cos(ac)], axis=1)
        sin = jnp.concatenate([-jnp.sin(ar), jnp.sin(ar), -jnp.sin(ac), jnp.sin(ac)], axis=1)
        cos = jnp.concatenate([jnp.ones((n_ctx, rot_dim), F32), cos], axis=0)
        sin = jnp.concatenate([jnp.zeros((n_ctx, rot_dim), F32), sin], axis=0)
        return cos, sin

    c64, s64 = lane_pattern(HEAD_DIM)
    c32, s32 = lane_pattern(DIFF_QK)
    t_all = n_ctx + n_lat
    rope_t = jnp.stack([jnp.tile(c64, (1, 2)), jnp.tile(s64, (1, 2)), jnp.tile(c32, (1, 4)), jnp.tile(s32, (1, 4))])
    ones, zeros = jnp.ones((MLA_NOPE, t_all), F32), jnp.zeros((MLA_NOPE, t_all), F32)
    pad1, pad0 = jnp.ones((32, t_all), F32), jnp.zeros((32, t_all), F32)
    rope_f = jnp.concatenate([c64.T, s64.T, c32.T, s32.T,
                              ones, c32.T, pad1, zeros, s32.T, pad0], axis=0)
    return rope_t, rope_f


def _layer_weights(l, w_in, gqa_q_norm, gqa_k_norm, mla_q_norm, mla_kv_norm, mla_w_uq, mla_w_ukv):
    w = w_in[l]
    col = lambda a, n: w[:, a:a + n]
    kpe = jnp.zeros((D_MODEL, 128), F32).at[:, 64:96].set(col(M0 + 384, 32))
    w_tok = jnp.concatenate([col(A0 + 256, 128), col(M0 + 256, 128), kpe, col(X0 + 256, 256),
                             col(S0 + 256, 128), col(G0, GATE_COLS)], axis=1)
    w_feat = jnp.concatenate([col(A0, 256), col(A0 + 384, 128), col(M0, 256), col(M0 + 256, 128),
                              col(X0, 256), col(X0 + 512, 256), col(S0, 256), col(S0 + 384, 128)], axis=1)
    uq = mla_w_uq[l].reshape(MLA_Q_RANK, MLA_HEADS, MLA_NOPE + MLA_ROPE)
    uq = jnp.pad(uq, ((0, 0), (0, 0), (0, 128 - MLA_NOPE - MLA_ROPE))).reshape(MLA_Q_RANK, 512)
    ukv = mla_w_ukv[l].reshape(MLA_KV_RANK, MLA_HEADS, 2 * MLA_NOPE)
    k_exp = jnp.pad(ukv[:, :, :MLA_NOPE], ((0, 0), (0, 0), (0, 64))).reshape(MLA_KV_RANK, 512)
    v_t = ukv[:, :, MLA_NOPE:].reshape(MLA_KV_RANK, 256).T
    return {
        "w_tok": w_tok.astype(MXU_DTYPE), "w_feat_t": w_feat.T.astype(MXU_DTYPE),
        "w_uq_t": uq.T.astype(MXU_DTYPE), "w_k_exp": k_exp.astype(MXU_DTYPE), "w_v_t": v_t.astype(MXU_DTYPE),
        "gk_row": jnp.tile(gqa_k_norm[l], 2)[None, :], "gkv_row": mla_kv_norm[l][None, :],
        "gq_col": gqa_q_norm[l][:, None], "gmq_col": mla_q_norm[l][:, None], "gkv_col": mla_kv_norm[l][:, None],
    }


def kernel(x, c, ctx, c_ctx, w_mod, b_mod, g_pre, g_post, w_ffn_gate, w_ffn_up, w_ffn_down, w_in,
           gqa_q_norm, gqa_k_norm, mla_q_norm, mla_kv_norm, mla_w_uq, mla_w_ukv,
           diff_lambda, diff_subln, swa_sink, w_branch, w_out):
    bsz, n_lat, d = x.shape
    n_ctx = ctx.shape[1]
    assert d == D_MODEL and n_ctx == TM and n_lat % (2 * TK) == 0 and bsz + 1 <= MOD_ROWS and bsz % SPS == 0
    n_ctx_tiles = n_ctx // TM

    cvec = jnp.zeros((MOD_ROWS, d), F32).at[:bsz].set(c).at[bsz].set(c_ctx)
    mod = _modulation(cvec, w_mod, b_mod).reshape(DEPTH, MOD_ROWS, 9, d)
    modsel = jnp.stack([jnp.broadcast_to(mod[:, bsz:bsz + 1], (DEPTH, bsz, 9, d)), mod[:, :bsz]], axis=2)

    rope_t, rope_f = _rope_tables(n_ctx, n_lat)
    xs = (ctx, x)
    cast = lambda a: a.astype(MXU_DTYPE)

    for l in range(DEPTH):
        ffn_w = lambda i: (cast(w_ffn_gate[l, i]), cast(w_ffn_up[l, i]), cast(w_ffn_down[l, i]))
        xs = _ffn_half(xs, modsel[l], g_pre[l, 0][None], g_post[l, 0][None], *ffn_w(0),
                       sub=0, n_ctx_tiles=n_ctx_tiles)
        w = _layer_weights(l, w_in, gqa_q_norm, gqa_k_norm, mla_q_norm, mla_kv_norm, mla_w_uq, mla_w_ukv)
        (k_a, k_m, k_d, k_s, gates, q_a, vt_a, q_m, vt_m, q_d, vt_d, q_s, vt_s) = _inproj(
            xs, modsel[l], g_pre[l, 1][None], w, rope_t, rope_f, n_ctx_tiles=n_ctx_tiles)
        lam_init = 0.8 - 0.6 * math.exp(-0.3 * l)
        lam_tab = jnp.zeros((8, 128), F32).at[0:4, 0:DIFF_QK].set(diff_lambda[l]).at[4, :].set(lam_init)
        sink_row = jnp.repeat(swa_sink[l], TM)[None, :]
        outs = (_dense_attention(q_a, k_a, vt_a, kind="gqa", n_ctx=n_ctx),
                _dense_attention(q_m, k_m, vt_m, kind="mla", n_ctx=n_ctx),
                _dense_attention(q_d, k_d, vt_d, kind="diff", n_ctx=n_ctx,
                                 lam_tab=lam_tab, subln_col=diff_subln[l][:, None]),
                _window_attention(q_s, k_s, vt_s, sink_row, n_ctx=n_ctx))
        xs = _merge_ffn(xs, modsel[l], g_post[l, 1][None], outs, gates, cast(w_branch[l]), cast(w_out[l]),
                        g_pre[l, 2][None], g_post[l, 2][None], *ffn_w(1), n_ctx_tiles=n_ctx_tiles,
                        latent_out=(l == DEPTH - 1))
    return xs
```
